```python
import math
import jax, jax.numpy as jnp
from jax import lax
import numpy as np

D_MODEL = 1024
BATCH = 8
SEQ = 4096
DEPTH = 1

N_META = 16
BLOCK = 128
WINDOW = 128
HEAD_DIM = 64
A_HEADS = D_MODEL // 256
A_V_DIM = 2 * HEAD_DIM
A_WIDTH = A_HEADS * A_V_DIM
B_HEADS = D_MODEL // 128
B_KV_HEADS = max(1, B_HEADS // 4)
B_GROUP = B_HEADS // B_KV_HEADS
B_WIDTH = B_HEADS * HEAD_DIM
D_FF = -(-8 * D_MODEL // (3 * 256)) * 256
EPS = 1e-6
QA_W = A_HEADS * 2 * HEAD_DIM
KA_W = A_HEADS * 2 * HEAD_DIM
VA_W = A_WIDTH
QB_W = B_WIDTH
KB_W = B_KV_HEADS * HEAD_DIM
VB_W = B_KV_HEADS * HEAD_DIM
GATE_W = D_MODEL
PROJ_W = QA_W + KA_W + VA_W + QB_W + KB_W + VB_W + 2 * GATE_W

kernel_name = "hybrid_diffattn_swa_gated_encoder"


def rms_norm(x, g):
    xf = x.astype(jnp.float32)
    y = xf * lax.rsqrt(jnp.mean(xf * xf, axis=-1, keepdims=True) + EPS)
    return (y * g.astype(jnp.float32)).astype(x.dtype)


def alibi_slopes(n):
    return jnp.asarray(2.0 ** (-8.0 * np.arange(1, n + 1) / n), dtype=jnp.float32)


def lambda_init_for(layer):
    return 0.8 - 0.6 * math.exp(-0.3 * layer)


def diff_attn_rows(q1, q2, k1, k2, v, qpos, kpos, key_real, slopes, lam):
    scale = HEAD_DIM ** -0.5
    dist = jnp.abs(qpos[:, None] - kpos[None, :]).astype(jnp.float32) * key_real[None, :]
    bias = -slopes[:, None, None] * dist[None]
    s1 = jnp.einsum('bqhd,bkhd->bhqk', q1, k1).astype(jnp.float32) * scale + bias
    s2 = jnp.einsum('bqhd,bkhd->bhqk', q2, k2).astype(jnp.float32) * scale + bias
    attn = jax.nn.softmax(s1, axis=-1) - lam * jax.nn.softmax(s2, axis=-1)
    return jnp.einsum('bhqk,bkhe->bqhe', attn.astype(v.dtype), v)


def differential_attention(q, k, v, lam_q1, lam_k1, lam_q2, lam_k2, subln_g, lambda_init):
    Bn, L = q.shape[0], q.shape[1]
    S = L - N_META
    nb = S // BLOCK
    pos = jnp.arange(L)
    key_real = (pos >= N_META).astype(jnp.float32)
    slopes = alibi_slopes(A_HEADS)
    lam = (jnp.exp(jnp.sum(lam_q1.astype(jnp.float32) * lam_k1.astype(jnp.float32)))
           - jnp.exp(jnp.sum(lam_q2.astype(jnp.float32) * lam_k2.astype(jnp.float32)))
           + lambda_init)
    q1, q2 = q[..., 0, :], q[..., 1, :]
    k1, k2 = k[..., 0, :], k[..., 1, :]

    def attend(a, b, p):
        return diff_attn_rows(a, b, k1, k2, v, p, pos, key_real, slopes, lam)

    y_meta = attend(q1[:, :N_META], q2[:, :N_META], pos[:N_META])

    def to_blocks(t):
        return t[:, N_META:].reshape(Bn, nb, BLOCK, *t.shape[2:]).swapaxes(0, 1)

    y_real = lax.map(lambda a: attend(*a),
                     (to_blocks(q1), to_blocks(q2), pos[N_META:].reshape(nb, BLOCK)))
    y_real = y_real.swapaxes(0, 1).reshape(Bn, S, A_HEADS, A_V_DIM)
    y = jnp.concatenate([y_meta, y_real], axis=1)
    y = rms_norm(y, subln_g) * (1.0 - lambda_init)
    return y.reshape(Bn, L, A_WIDTH)


def window_attn_blocks(q, k, v, qpos, kpos, kvalid, kreal, slopes, sink):
    scale = HEAD_DIM ** -0.5
    dt = jnp.abs(qpos[:, :, None] - kpos[:, None, :])
    visible = kvalid[:, None, :] & ((~kreal[:, None, :]) | (dt <= WINDOW))
    dist = jnp.where(kreal[:, None, :], dt, 0).astype(jnp.float32)
    bias = -slopes[None, :, :, None, None] * dist[:, None, None]
    s = jnp.einsum('bnqgrd,bnkgd->bngrqk', q, k).astype(jnp.float32) * scale + bias
    s = jnp.where(visible[None, :, None, None], s, -jnp.inf)
    sink_b = sink.astype(jnp.float32)[None, None, :, :, None, None]
    m = jnp.maximum(jnp.max(s, axis=-1, keepdims=True), sink_b)
    p = jnp.exp(s - m)
    denom = jnp.sum(p, axis=-1, keepdims=True) + jnp.exp(sink_b - m)
    return jnp.einsum('bngrqk,bnkgd->bnqgrd', (p / denom).astype(v.dtype), v)


def windowed_gqa(q, k, v, sink):
    Bn, L = q.shape[0], q.shape[1]
    S = L - N_META
    nb = S // BLOCK
    G, d = B_KV_HEADS, HEAD_DIM
    slopes = alibi_slopes(B_HEADS).reshape(B_KV_HEADS, B_GROUP)

    def band(t):
        tp = jnp.pad(t[:, N_META:], ((0, 0), (BLOCK, BLOCK), (0, 0), (0, 0)))
        tp = tp.reshape(Bn, nb + 2, BLOCK, G, d)
        win = jnp.concatenate([tp[:, j:j + nb] for j in range(3)], axis=2)
        meta = jnp.broadcast_to(t[:, None, :N_META], (Bn, nb, N_META, G, d))
        return jnp.concatenate([meta, win], axis=2)

    blk = jnp.arange(nb)[:, None]
    win_idx = (blk - 1) * BLOCK + jnp.arange(3 * BLOCK)[None, :]
    meta_pos = jnp.broadcast_to(jnp.arange(N_META)[None, :], (nb, N_META))
    kpos = jnp.concatenate([meta_pos, N_META + win_idx], axis=1)
    kvalid = jnp.concatenate([jnp.ones((nb, N_META), bool),
                              (win_idx >= 0) & (win_idx < S)], axis=1)
    kreal = jnp.concatenate([jnp.zeros((nb, N_META), bool),
                             jnp.ones((nb, 3 * BLOCK), bool)], axis=1)
    qpos = N_META + jnp.arange(S).reshape(nb, BLOCK)
    q_real = q[:, N_META:].reshape(Bn, nb, BLOCK, G, B_GROUP, d)
    y_real = window_attn_blocks(q_real, band(k), band(v), qpos, kpos, kvalid, kreal, slopes, sink)
    y_real = y_real.reshape(Bn, S, B_WIDTH)

    tk = N_META + BLOCK
    kpos_m = jnp.arange(tk)[None, :]
    y_meta = window_attn_blocks(q[:, None, :N_META], k[:, None, :tk], v[:, None, :tk],
                                jnp.arange(N_META)[None, :], kpos_m,
                                jnp.ones((1, tk), bool), kpos_m >= N_META, slopes, sink)
    y_meta = y_meta.reshape(Bn, N_META, B_WIDTH)
    return jnp.concatenate([y_meta, y_real], axis=1)


def mixer_block(h, w_in, lq1, lk1, lq2, lk2, subln_g, sink, w_ba, w_bb, w_o, lambda_init):
    Bn, L = h.shape[0], h.shape[1]
    proj = h @ w_in
    o = 0
    qa = proj[..., o:o + QA_W]; o += QA_W
    ka = proj[..., o:o + KA_W]; o += KA_W
    va = proj[..., o:o + VA_W]; o += VA_W
    qb = proj[..., o:o + QB_W]; o += QB_W
    kb = proj[..., o:o + KB_W]; o += KB_W
    vb = proj[..., o:o + VB_W]; o += VB_W
    ga = proj[..., o:o + GATE_W]; o += GATE_W
    gb = proj[..., o:o + GATE_W]
    y_a = differential_attention(qa.reshape(Bn, L, A_HEADS, 2, HEAD_DIM),
                                 ka.reshape(Bn, L, A_HEADS, 2, HEAD_DIM),
                                 va.reshape(Bn, L, A_HEADS, A_V_DIM),
                                 lq1, lk1, lq2, lk2, subln_g, lambda_init)
    y_b = windowed_gqa(qb.reshape(Bn, L, B_KV_HEADS, B_GROUP, HEAD_DIM),
                       kb.reshape(Bn, L, B_KV_HEADS, HEAD_DIM),
                       vb.reshape(Bn, L, B_KV_HEADS, HEAD_DIM),
                       sink.reshape(B_KV_HEADS, B_GROUP))
    merged = jax.nn.sigmoid(ga) * (y_a @ w_ba) + jax.nn.sigmoid(gb) * (y_b @ w_bb)
    return merged @ w_o


def swiglu(h, w_gate, w_up, w_down):
    return (jax.nn.silu(h @ w_gate) * (h @ w_up)) @ w_down


def setup_inputs(seed: int = 0) -> dict:
    key = jax.random.key(seed)
    ks = jax.random.split(key, 20)

    def w(k, shape, fan_in):
        return jax.random.normal(k, shape, jnp.float32) * (fan_in ** -0.5)

    def gain(k, shape):
        return 1.0 + 0.02 * jax.random.normal(k, shape, jnp.float32)

    return {
        "x": jax.random.normal(ks[0], (BATCH, SEQ, D_MODEL), jnp.float32),
        "meta_tokens": jax.random.normal(ks[1], (N_META, D_MODEL), jnp.float32),
        "norm_mix": gain(ks[2], (DEPTH, D_MODEL)),
        "w_in": w(ks[3], (DEPTH, D_MODEL, PROJ_W), D_MODEL),
        "lambda_q1": 0.1 * jax.random.normal(ks[4], (DEPTH, HEAD_DIM), jnp.float32),
        "lambda_k1": 0.1 * jax.random.normal(ks[5], (DEPTH, HEAD_DIM), jnp.float32),
        "lambda_q2": 0.1 * jax.random.normal(ks[6], (DEPTH, HEAD_DIM), jnp.float32),
        "lambda_k2": 0.1 * jax.random.normal(ks[7], (DEPTH, HEAD_DIM), jnp.float32),
        "subln_gain": gain(ks[8], (DEPTH, A_V_DIM)),
        "sink_logits": 0.5 * jax.random.normal(ks[9], (DEPTH, B_HEADS), jnp.float32),
        "w_branch_a": w(ks[10], (DEPTH, A_WIDTH, D_MODEL), A_WIDTH),
        "w_branch_b": w(ks[11], (DEPTH, B_WIDTH, D_MODEL), B_WIDTH),
        "w_out": w(ks[12], (DEPTH, D_MODEL, D_MODEL), D_MODEL),
        "norm_ffn": gain(ks[13], (DEPTH, D_MODEL)),
        "w_ff_gate": w(ks[14], (DEPTH, D_MODEL, D_FF), D_MODEL),
        "w_ff_up": w(ks[15], (DEPTH, D_MODEL, D_FF), D_MODEL),
        "w_ff_down": w(ks[16], (DEPTH, D_FF, D_MODEL), D_FF),
        "norm_final": gain(ks[17], (D_MODEL,)),
    }


def reference(x, meta_tokens, norm_mix, w_in, lambda_q1, lambda_k1, lambda_q2, lambda_k2,
              subln_gain, sink_logits, w_branch_a, w_branch_b, w_out, norm_ffn,
              w_ff_gate, w_ff_up, w_ff_down, norm_final):
    Bn = x.shape[0]
    meta = jnp.broadcast_to(meta_tokens.astype(x.dtype)[None], (Bn, N_META, x.shape[-1]))
    h = jnp.concatenate([meta, x], axis=1)
    for layer in range(DEPTH):
        h = h + mixer_block(rms_norm(h, norm_mix[layer]), w_in[layer],
                            lambda_q1[layer], lambda_k1[layer], lambda_q2[layer], lambda_k2[layer],
                            subln_gain[layer], sink_logits[layer],
                            w_branch_a[layer], w_branch_b[layer], w_out[layer],
                            lambda_init_for(layer))
        h = h + swiglu(rms_norm(h, norm_ffn[layer]), w_ff_gate[layer], w_ff_up[layer], w_ff_down[layer])
    return rms_norm(h, norm_final)[:, N_META:]
```

```python
import functools
import math

import jax
import jax.numpy as jnp
from jax import lax
from jax.experimental import pallas as pl
from jax.experimental.pallas import tpu as pltpu

F32 = jnp.float32
BF16 = jnp.bfloat16

N_META = 16
HEAD_DIM = 64
WINDOW = 128
BLOCK = 128
A_V_DIM = 2 * HEAD_DIM
EPS = 1e-6
LAMBDA_INIT = 0.8 - 0.6 * math.exp(-0.3 * 0)
SCALE = HEAD_DIM ** -0.5

LANES = 128
META_PAD = 128
NEG = -1e30
VMEM_LIMIT_BYTES = 56 * 1024 * 1024

_DN_NT = (((1,), (1,)), ((), ()))


def _resident(block_shape, index_map):
    return pl.BlockSpec(block_shape, index_map, pipeline_mode=pl.Buffered(1))


def _in_proj_kernel(x_ref, g_ref, w_ref, *out_refs, splits, with_gates):
    x = x_ref[...]
    ms = jnp.mean(x * x, axis=-1, keepdims=True)
    hn = ((x * lax.rsqrt(ms + EPS)) * g_ref[...]).astype(BF16)
    for (lo, hi), o_ref in zip(splits, out_refs):
        o_ref[...] = jnp.dot(hn, w_ref[:, lo:hi], preferred_element_type=F32).astype(BF16)
    if with_gates:
        lo = splits[-1][1]
        g = jnp.dot(hn, w_ref[:, lo:], preferred_element_type=F32)
        out_refs[-1][...] = jax.nn.sigmoid(g).astype(BF16)


def _in_proj(x2d, gain, w_bf16, splits, with_gates, tm):
    m, d = x2d.shape
    n_all = w_bf16.shape[1]
    widths = [hi - lo for lo, hi in splits]
    if with_gates:
        widths.append(n_all - splits[-1][1])
    out_shape = [jax.ShapeDtypeStruct((m, w), BF16) for w in widths]
    out_specs = [pl.BlockSpec((tm, w), lambda i: (i, 0)) for w in widths]
    return pl.pallas_call(
        functools.partial(_in_proj_kernel, splits=splits, with_gates=with_gates),
        grid=(m // tm,),
        in_specs=[
            pl.BlockSpec((tm, d), lambda i: (i, 0)),
            _resident((1, d), lambda i: (0, 0)),
            _resident((d, n_all), lambda i: (0, 0)),
        ],
        out_specs=out_specs,
        out_shape=out_shape,
        compiler_params=pltpu.CompilerParams(
            dimension_semantics=("arbitrary",), vmem_limit_bytes=VMEM_LIMIT_BYTES),
        name="in_proj_gates" if with_gates else "in_proj_meta",
    )(x2d, gain, w_bf16)


def _attn_a_kernel(slopes_ref, lq1_ref, lk1_ref, lq2_ref, lk2_ref, gain_ref,
                   q_ref, k_ref, v_ref, km_ref, vm_ref, o_ref, *, tq, tk, seq):
    h = pl.program_id(1)
    qi = pl.program_id(2)
    slope = slopes_ref[h]

    q = q_ref[...] * SCALE
    lane = lax.broadcasted_iota(jnp.int32, q.shape, 1)
    zero = jnp.zeros_like(q)
    q1 = jnp.where(lane < HEAD_DIM, q, zero)
    q2 = jnp.where(lane >= HEAD_DIM, q, zero)

    km = km_ref[...]
    vm = vm_ref[...]
    meta_valid = lax.broadcasted_iota(jnp.int32, (tq, META_PAD), 1) < N_META

    def first(qx):
        s = lax.dot_general(qx, km, _DN_NT, preferred_element_type=F32)
        s = jnp.where(meta_valid, s, NEG)
        m = jnp.max(s, axis=-1, keepdims=True)
        p = jnp.exp(s - m)
        l = jnp.sum(p, axis=-1, keepdims=True)
        acc = jnp.dot(p.astype(BF16), vm, preferred_element_type=F32)
        return m, l, acc

    rel = (lax.broadcasted_iota(jnp.int32, (tq, tk), 0)
           - lax.broadcasted_iota(jnp.int32, (tq, tk), 1))

    def step(qx, kc, vc, bias, m, l, acc):
        s = lax.dot_general(qx, kc, _DN_NT, preferred_element_type=F32) + bias
        m_new = jnp.maximum(m, jnp.max(s, axis=-1, keepdims=True))
        alpha = jnp.exp(m - m_new)
        p = jnp.exp(s - m_new)
        l = alpha * l + jnp.sum(p, axis=-1, keepdims=True)
        acc = alpha * acc + jnp.dot(p.astype(BF16), vc, preferred_element_type=F32)
        return m_new, l, acc

    def body(c, carry):
        m1, l1, a1, m2, l2, a2 = carry
        start = pl.multiple_of(c * tk, tk)
        kc = k_ref[pl.ds(start, tk), :]
        vc = v_ref[pl.ds(start, tk), :]
        dist = jnp.abs(rel + (qi * tq - c * tk)).astype(F32)
        bias = dist * (-slope)
        m1, l1, a1 = step(q1, kc, vc, bias, m1, l1, a1)
        m2, l2, a2 = step(q2, kc, vc, bias, m2, l2, a2)
        return m1, l1, a1, m2, l2, a2

    m1, l1, a1, m2, l2, a2 = lax.fori_loop(0, seq // tk, body, first(q1) + first(q2))

    lam = (jnp.exp(jnp.sum(lq1_ref[...] * lk1_ref[...], axis=-1, keepdims=True))
           - jnp.exp(jnp.sum(lq2_ref[...] * lk2_ref[...], axis=-1, keepdims=True))
           + LAMBDA_INIT)
    y = a1 / l1 - lam * (a2 / l2)
    ms = jnp.mean(y * y, axis=-1, keepdims=True)
    y = ((y * lax.rsqrt(ms + EPS)) * gain_ref[...]) * (1.0 - LAMBDA_INIT)
    o_ref[...] = y.astype(BF16)


def _attn_a(pa, pa_meta, slopes, lq1, lk1, lq2, lk2, gain, batch, seq, heads, tq, tk):
    nq = seq // tq
    width = heads * A_V_DIM
    vec = lambda n: _resident((1, n), lambda b, h, i: (0, 0))
    return pl.pallas_call(
        functools.partial(_attn_a_kernel, tq=tq, tk=tk, seq=seq),
        grid=(batch, heads, nq),
        in_specs=[
            pl.BlockSpec(memory_space=pltpu.SMEM),
            vec(HEAD_DIM), vec(HEAD_DIM), vec(HEAD_DIM), vec(HEAD_DIM), vec(A_V_DIM),
            pl.BlockSpec((tq, LANES), lambda b, h, i: (b * nq + i, h)),
            pl.BlockSpec((seq, LANES), lambda b, h, i: (b, heads + h)),
            pl.BlockSpec((seq, LANES), lambda b, h, i: (b, 2 * heads + h)),
            pl.BlockSpec((META_PAD, LANES), lambda b, h, i: (0, heads + h)),
            pl.BlockSpec((META_PAD, LANES), lambda b, h, i: (0, 2 * heads + h)),
        ],
        out_specs=pl.BlockSpec((tq, A_V_DIM), lambda b, h, i: (b * nq + i, h)),
        out_shape=jax.ShapeDtypeStruct((batch * seq, width), BF16),
        compiler_params=pltpu.CompilerParams(
            dimension_semantics=("arbitrary", "arbitrary", "arbitrary"),
            vmem_limit_bytes=VMEM_LIMIT_BYTES),
        name="attn_a",
    )(slopes, lq1, lk1, lq2, lk2, gain, pa, pa, pa, pa_meta, pa_meta)


def _swap_halves(x):
    return jnp.concatenate([x[:, HEAD_DIM:], x[:, :HEAD_DIM]], axis=1)


def _attn_b_kernel(sink_ref, q_ref, kp_ref, kc_ref, kn_ref, vp_ref, vc_ref, vn_ref,
                   km_ref, vm_ref, o_ref, *, seq, heads, group):
    i = pl.program_id(1)
    nk = META_PAD + 3 * BLOCK
    kk = jnp.concatenate([km_ref[...], kp_ref[...], kc_ref[...], kn_ref[...]], axis=0)
    vv = jnp.concatenate([vm_ref[...], vp_ref[...], vc_ref[...], vn_ref[...]], axis=0)
    kk_sw = _swap_halves(kk)
    vv_sw = _swap_halves(vv)

    col = lax.broadcasted_iota(jnp.int32, (BLOCK, nk), 1)
    row = lax.broadcasted_iota(jnp.int32, (BLOCK, nk), 0)
    is_meta = col < META_PAD
    kpos_rel = col - (META_PAD + BLOCK)
    dt = jnp.abs(row - kpos_rel)
    kabs = i * BLOCK + kpos_rel
    vis = (col < N_META) | ((dt <= WINDOW) & (kabs >= 0) & (kabs < seq))
    negd = jnp.where(is_meta, 0.0, -dt.astype(F32))
    negd = jnp.where(vis, negd, NEG)

    q = q_ref[...] * SCALE
    lane = lax.broadcasted_iota(jnp.int32, (BLOCK, LANES), 1)
    lo = lane < HEAD_DIM
    zero = jnp.zeros((BLOCK, LANES), BF16)

    def masked_q(hd):
        blk = q[:, (hd // 2) * LANES:(hd // 2 + 1) * LANES]
        return jnp.where(lo if hd % 2 == 0 else ~lo, blk, zero)

    same = [hd for hd in range(heads) if (hd % 2) == (hd // group)]
    swap = [hd for hd in range(heads) if (hd % 2) != (hd // group)]

    def run(hds, keys, vals):
        qs = jnp.concatenate([masked_q(hd) for hd in hds], axis=0)
        s_all = lax.dot_general(qs, keys, _DN_NT, preferred_element_type=F32)
        ps, invs = [], []
        for r, hd in enumerate(hds):
            slope = 2.0 ** (-8.0 * (hd + 1) / heads)
            sink = sink_ref[hd]
            s = s_all[r * BLOCK:(r + 1) * BLOCK] + slope * negd
            m = jnp.maximum(jnp.max(s, axis=-1, keepdims=True), sink)
            p = jnp.exp(s - m)
            denom = jnp.sum(p, axis=-1, keepdims=True) + jnp.exp(sink - m)
            ps.append(p.astype(BF16))
            invs.append(1.0 / denom)
        out = jnp.dot(jnp.concatenate(ps, axis=0), vals, preferred_element_type=F32)
        return {hd: out[r * BLOCK:(r + 1) * BLOCK] * invs[r] for r, hd in enumerate(hds)}

    res = run(same, kk, vv)
    res.update(run(swap, kk_sw, vv_sw))
    blocks = []
    for j in range(heads // 2):
        blocks.append(jnp.where(lo, res[2 * j], res[2 * j + 1]))
    o_ref[...] = jnp.concatenate(blocks, axis=1).astype(BF16)


def _attn_b(pb, pb_meta, sink, batch, seq, heads, kv_heads):
    nb = seq // BLOCK
    qw = heads * HEAD_DIM
    kcol = qw // LANES
    vcol = kcol + 1
    assert kv_heads * HEAD_DIM == LANES
    blk = lambda off: (lambda b, i: (b * nb + jnp.clip(i + off, 0, nb - 1), 0))
    kv = lambda off, c: pl.BlockSpec(
        (BLOCK, LANES), lambda b, i: (b * nb + jnp.clip(i + off, 0, nb - 1), c))
    meta = lambda c: _resident((META_PAD, LANES), lambda b, i: (0, c))
    return pl.pallas_call(
        functools.partial(_attn_b_kernel, seq=seq, heads=heads, group=heads // kv_heads),
        grid=(batch, nb),
        in_specs=[
            pl.BlockSpec(memory_space=pltpu.SMEM),
            pl.BlockSpec((BLOCK, qw), blk(0)),
            kv(-1, kcol), kv(0, kcol), kv(1, kcol),
            kv(-1, vcol), kv(0, vcol), kv(1, vcol),
            meta(kcol), meta(vcol),
        ],
        out_specs=pl.BlockSpec((BLOCK, qw), blk(0)),
        out_shape=jax.ShapeDtypeStruct((batch * seq, qw), BF16),
        compiler_params=pltpu.CompilerParams(
            dimension_semantics=("arbitrary", "arbitrary"),
            vmem_limit_bytes=VMEM_LIMIT_BYTES),
        name="attn_b",
    )(sink, pb, pb, pb, pb, pb, pb, pb, pb_meta, pb_meta)


def _rms(x, g):
    ms = jnp.mean(x * x, axis=-1, keepdims=True)
    return (x * lax.rsqrt(ms + EPS)) * g


def _post_kernel(x_ref, ya_ref, yb_ref, gate_ref, wba_ref, wbb_ref, wo_ref, nf_ref,
                 wg_ref, wu_ref, wd_ref, nfin_ref, o_ref, *, d_model):
    dot = functools.partial(jnp.dot, preferred_element_type=F32)
    ga = gate_ref[:, :d_model].astype(F32)
    gb = gate_ref[:, d_model:].astype(F32)
    merged = ga * dot(ya_ref[...], wba_ref[...]) + gb * dot(yb_ref[...], wbb_ref[...])
    h1 = x_ref[...] + dot(merged.astype(BF16), wo_ref[...])
    hn = _rms(h1, nf_ref[...]).astype(BF16)
    act = jax.nn.silu(dot(hn, wg_ref[...])) * dot(hn, wu_ref[...])
    h2 = h1 + dot(act.astype(BF16), wd_ref[...])
    o_ref[...] = _rms(h2, nfin_ref[...])


def _post(x2d, ya, yb, gates, wba, wbb, wo, nf, wg, wu, wd, nfin, tm):
    m, d = x2d.shape
    dff = wg.shape[1]
    row = lambda w: pl.BlockSpec((tm, w), lambda i: (i, 0))
    full = lambda a: _resident(a.shape, lambda i: (0, 0))
    return pl.pallas_call(
        functools.partial(_post_kernel, d_model=d),
        grid=(m // tm,),
        in_specs=[row(d), row(ya.shape[1]), row(yb.shape[1]), row(2 * d),
                  full(wba), full(wbb), full(wo), full(nf),
                  full(wg), full(wu), full(wd), full(nfin)],
        out_specs=row(d),
        out_shape=jax.ShapeDtypeStruct((m, d), F32),
        compiler_params=pltpu.CompilerParams(
            dimension_semantics=("arbitrary",), vmem_limit_bytes=VMEM_LIMIT_BYTES),
        name="post_ffn",
    )(x2d, ya, yb, gates, wba, wbb, wo, nf, wg, wu, wd, nfin)


def kernel(x, meta_tokens, norm_mix, w_in, lambda_q1, lambda_k1, lambda_q2, lambda_k2,
           subln_gain, sink_logits, w_branch_a, w_branch_b, w_out, norm_ffn,
           w_ff_gate, w_ff_up, w_ff_down, norm_final):
    batch, seq, d = x.shape
    assert norm_mix.shape[0] == 1, "single-layer block"
    a_heads = w_branch_a.shape[1] // A_V_DIM
    b_heads = sink_logits.shape[1]
    b_kv = max(1, b_heads // 4)
    qa_w = a_heads * 2 * HEAD_DIM
    pa_w = 2 * qa_w + a_heads * A_V_DIM
    pb_w = b_heads * HEAD_DIM + 2 * b_kv * HEAD_DIM
    splits = ((0, pa_w), (pa_w, pa_w + pb_w))

    x2d = x.reshape(batch * seq, d)
    w_in_b = w_in[0].astype(BF16)
    gain = norm_mix[0][None, :]

    pa, pb, gates = _in_proj(x2d, gain, w_in_b, splits, True, tm=512)
    pa_m, pb_m = _in_proj(meta_tokens.astype(F32), gain, w_in_b, splits, False, tm=N_META)
    pad = ((0, META_PAD - N_META), (0, 0))
    pa_m = jnp.pad(pa_m, pad)
    pb_m = jnp.pad(pb_m, pad)

    slopes = jnp.asarray([2.0 ** (-8.0 * (i + 1) / a_heads) for i in range(a_heads)], F32)
    ya = _attn_a(pa, pa_m, slopes, lambda_q1, lambda_k1, lambda_q2, lambda_k2,
                 subln_gain, batch, seq, a_heads, tq=512, tk=512)
    yb = _attn_b(pb, pb_m, sink_logits[0], batch, seq, b_heads, b_kv)

    out = _post(x2d, ya, yb, gates,
                w_branch_a[0].astype(BF16), w_branch_b[0].astype(BF16), w_out[0].astype(BF16),
                norm_ffn[0][None, :], w_ff_gate[0].astype(BF16), w_ff_up[0].astype(BF16),
                w_ff_down[0].astype(BF16), norm_final[None, :], tm=256)
    return out.reshape(batch, seq, d)
```

```python
import functools
import math

import jax
import jax.numpy as jnp
from jax import lax
from jax.experimental import pallas as pl
from jax.experimental.pallas import tpu as pltpu

F32 = jnp.float32
BF16 = jnp.bfloat16

N_META = 16
HEAD_DIM = 64
WINDOW = 128
BLOCK = 128
A_V_DIM = 2 * HEAD_DIM
EPS = 1e-6
LAMBDA_INIT = 0.8 - 0.6 * math.exp(-0.3 * 0)
SCALE = HEAD_DIM ** -0.5

LANES = 128
META_PAD = 128
NEG = -1e30
VMEM_LIMIT_BYTES = 56 * 1024 * 1024

A_KEY_CHUNK = 512
A_QUERY_TILE = 256
POST_ROW_TILE = 256

_DN_NT = (((1,), (1,)), ((), ()))


def _resident(block_shape, index_map):
    return pl.BlockSpec(block_shape, index_map, pipeline_mode=pl.Buffered(1))


def _in_proj_kernel(x_ref, g_ref, w_ref, wvt_ref, *out_refs, splits, gates_from):
    x = x_ref[...]
    ms = jnp.mean(x * x, axis=-1, keepdims=True)
    hn = ((x * lax.rsqrt(ms + EPS)) * g_ref[...]).astype(BF16)
    for (lo, hi), o_ref in zip(splits, out_refs):
        o_ref[...] = jnp.dot(hn, w_ref[:, lo:hi], preferred_element_type=F32).astype(BF16)
    if gates_from is not None:
        vt_ref, gate_ref = out_refs[len(splits):]
        vt = lax.dot_general(wvt_ref[...], hn, _DN_NT, preferred_element_type=F32)
        vt_ref[0] = vt.astype(BF16)
        g = jnp.dot(hn, w_ref[:, gates_from:], preferred_element_type=F32)
        gate_ref[...] = jax.nn.sigmoid(g).astype(BF16)


def _in_proj(x2d, gain, w_bf16, wvt_bf16, splits, gates_from, tm):
    m, d = x2d.shape
    n_all = w_bf16.shape[1]
    nv = wvt_bf16.shape[0]
    widths = [hi - lo for lo, hi in splits]
    out_shape = [jax.ShapeDtypeStruct((m, w), BF16) for w in widths]
    out_specs = [pl.BlockSpec((tm, w), lambda i: (i, 0)) for w in widths]
    if gates_from is not None:
        out_shape += [jax.ShapeDtypeStruct((m // tm, nv, tm), BF16),
                      jax.ShapeDtypeStruct((m, n_all - gates_from), BF16)]
        out_specs += [pl.BlockSpec((1, nv, tm), lambda i: (i, 0, 0)),
                      pl.BlockSpec((tm, n_all - gates_from), lambda i: (i, 0))]
    return pl.pallas_call(
        functools.partial(_in_proj_kernel, splits=splits, gates_from=gates_from),
        grid=(m // tm,),
        in_specs=[
            pl.BlockSpec((tm, d), lambda i: (i, 0)),
            _resident((1, d), lambda i: (0, 0)),
            _resident((d, n_all), lambda i: (0, 0)),
            _resident((nv, d), lambda i: (0, 0)),
        ],
        out_specs=out_specs,
        out_shape=out_shape,
        compiler_params=pltpu.CompilerParams(
            dimension_semantics=("arbitrary",), vmem_limit_bytes=VMEM_LIMIT_BYTES),
        name="in_proj_meta" if gates_from is None else "in_proj_gates",
    )(x2d, gain, w_bf16, wvt_bf16)


_FEAT_SPLIT_BITS = 6


def _pos_features(pos, slope):
    hi = (pos >> _FEAT_SPLIT_BITS).astype(F32) * (slope * (1 << _FEAT_SPLIT_BITS))
    lo = (pos & ((1 << _FEAT_SPLIT_BITS) - 1)).astype(F32) * slope
    return hi, lo


def _attn_a_kernel(slopes_ref, lq1_ref, lk1_ref, lq2_ref, lk2_ref, gain_ref,
                   q_ref, k_ref, vt_ref, km_ref, vtm_ref, o_ref,
                   kaug_ref, s_buf, p_buf, acc_ref, *, tq, tk, seq):
    h = pl.program_id(1)
    qi = pl.program_id(2)
    slope = slopes_ref[h]
    nqc = 2 * tq
    n_chunks = seq // tk

    @pl.when(qi == 0)
    def _build_augmented_keys():
        def fill(c, carry):
            rows = pl.ds(pl.multiple_of(c * tk, tk), tk)
            kaug_ref[rows, :LANES] = k_ref[rows, :]
            j = c * tk + lax.broadcasted_iota(jnp.int32, (tk, LANES), 0)
            lane = lax.broadcasted_iota(jnp.int32, (tk, LANES), 1)
            hi, lo = _pos_features(j, slope)
            f = jnp.where(lane == 0, hi, jnp.where(lane == 1, lo, jnp.where(lane < 4, 1.0, 0.0)))
            kaug_ref[rows, LANES:] = f.astype(BF16)
            return carry
        lax.fori_loop(0, n_chunks, fill, 0)

    q = q_ref[...] * SCALE
    lane = lax.broadcasted_iota(jnp.int32, (tq, LANES), 1)
    zero = jnp.zeros_like(q)
    qq = jnp.concatenate([jnp.where(lane < HEAD_DIM, q, zero),
                          jnp.where(lane >= HEAD_DIM, q, zero)], axis=0)
    lane2 = lax.broadcasted_iota(jnp.int32, (nqc, LANES), 1)
    i_pos = qi * tq + (lax.broadcasted_iota(jnp.int32, (nqc, LANES), 0) & (tq - 1))
    hi, lo = _pos_features(i_pos, slope)
    qf = jnp.where(lane2 < 2, 1.0, jnp.where(lane2 == 2, -hi, jnp.where(lane2 == 3, -lo, 0.0)))
    qf = qf.astype(BF16)
    q_left = jnp.concatenate([qq, qf], axis=1)
    q_right = jnp.concatenate([qq, -qf], axis=1)
    q_plain = jnp.concatenate([qq, jnp.zeros_like(qf)], axis=1)

    def scores(k_rows, q_aug):
        return lax.dot_general(k_rows, q_aug, _DN_NT, preferred_element_type=F32)

    km_aug = jnp.concatenate([km_ref[...], jnp.zeros((META_PAD, LANES), BF16)], axis=1)
    meta_valid = lax.broadcasted_iota(jnp.int32, (META_PAD, nqc), 0) < N_META
    s = jnp.where(meta_valid, scores(km_aug, q_plain), NEG)
    m = jnp.max(s, axis=0, keepdims=True)
    p = jnp.exp(s - m)
    l = jnp.sum(p, axis=0, keepdims=True)
    acc_ref[...] = jnp.dot(vtm_ref[...], p.astype(BF16), preferred_element_type=F32)

    c_diag = (qi * tq) // tk

    def chunk_of(t):
        u = t - 1
        return jnp.where(t == 0, c_diag, u + (u >= c_diag).astype(jnp.int32))

    def stage_scores(t):
        c = chunk_of(t)
        rows = pl.ds(pl.multiple_of(c * tk, tk), tk)
        return scores(kaug_ref[rows, :], jnp.where(c < c_diag, q_left, q_right))

    def stage_softmax(s, m, l):
        m_new = jnp.maximum(m, jnp.max(s, axis=0, keepdims=True))
        alpha = jnp.exp(m - m_new)
        p = jnp.exp(s - m_new)
        l = alpha * l + jnp.sum(p, axis=0, keepdims=True)
        return m_new, l, alpha, p.astype(BF16)

    def stage_pv(t, p, alpha):
        acc_ref[...] = alpha * acc_ref[...] + jnp.dot(vt_ref[chunk_of(t)], p,
                                                      preferred_element_type=F32)

    diag = pl.ds(pl.multiple_of(c_diag * tk, tk), tk)
    d = ((lax.broadcasted_iota(jnp.int32, (tk, nqc), 1) & (tq - 1))
         - lax.broadcasted_iota(jnp.int32, (tk, nqc), 0) + (qi * tq - c_diag * tk))
    s0 = scores(kaug_ref[diag, :], q_plain) - jnp.abs(d).astype(F32) * slope
    m, l, alpha, p = stage_softmax(s0, m, l)
    p_buf[0] = p
    s_buf[1] = stage_scores(1)

    def step(u, slot, m, l, alpha):
        stage_pv(u, p_buf[slot], alpha)
        m, l, alpha, p = stage_softmax(s_buf[1 - slot], m, l)
        p_buf[1 - slot] = p
        s_buf[slot] = stage_scores(u + 2)
        return m, l, alpha

    for u in range(n_chunks - 2):
        m, l, alpha = step(u, u % 2, m, l, alpha)
    stage_pv(n_chunks - 2, p_buf[n_chunks % 2], alpha)
    m, l, alpha, p = stage_softmax(s_buf[(n_chunks - 1) % 2], m, l)
    stage_pv(n_chunks - 1, p, alpha)
    acc = acc_ref[...]

    lam = (jnp.exp(jnp.sum(lq1_ref[...] * lk1_ref[...], axis=-1, keepdims=True))
           - jnp.exp(jnp.sum(lq2_ref[...] * lk2_ref[...], axis=-1, keepdims=True))
           + LAMBDA_INIT)
    o = acc / l
    y = o[:, :tq] - lam * o[:, tq:]
    ms = jnp.mean(y * y, axis=0, keepdims=True)
    y = ((y * lax.rsqrt(ms + EPS)) * gain_ref[...]) * (1.0 - LAMBDA_INIT)
    o_ref[...] = y.T.astype(BF16)


def _attn_a(pa, vt, pa_meta, vt_meta, slopes, lq1, lk1, lq2, lk2, gain_col,
            batch, seq, heads, tq, tk):
    nq = seq // tq
    width = heads * A_V_DIM
    vec = lambda n: _resident((1, n), lambda b, h, i: (0, 0))
    return pl.pallas_call(
        functools.partial(_attn_a_kernel, tq=tq, tk=tk, seq=seq),
        grid=(batch, heads, nq),
        in_specs=[
            pl.BlockSpec(memory_space=pltpu.SMEM),
            vec(HEAD_DIM), vec(HEAD_DIM), vec(HEAD_DIM), vec(HEAD_DIM),
            _resident((A_V_DIM, 1), lambda b, h, i: (0, 0)),
            pl.BlockSpec((tq, LANES), lambda b, h, i: (b * nq + i, h)),
            pl.BlockSpec((seq, LANES), lambda b, h, i: (b, heads + h)),
            pl.BlockSpec((seq // tk, A_V_DIM, tk), lambda b, h, i: (b, h, 0)),
            pl.BlockSpec((META_PAD, LANES), lambda b, h, i: (0, heads + h)),
            pl.BlockSpec((A_V_DIM, META_PAD), lambda b, h, i: (h, 0)),
        ],
        out_specs=pl.BlockSpec((tq, A_V_DIM), lambda b, h, i: (b * nq + i, h)),
        out_shape=jax.ShapeDtypeStruct((batch * seq, width), BF16),
        scratch_shapes=[pltpu.VMEM((seq, 2 * LANES), BF16),
                        pltpu.VMEM((2, tk, 2 * tq), F32),
                        pltpu.VMEM((2, tk, 2 * tq), BF16),
                        pltpu.VMEM((A_V_DIM, 2 * tq), F32)],
        compiler_params=pltpu.CompilerParams(
            dimension_semantics=("arbitrary", "arbitrary", "arbitrary"),
            vmem_limit_bytes=VMEM_LIMIT_BYTES),
        name="attn_a",
    )(slopes, lq1, lk1, lq2, lk2, gain_col, pa, pa, vt, pa_meta, vt_meta)


def _swap_halves(x):
    return jnp.concatenate([x[:, HEAD_DIM:], x[:, :HEAD_DIM]], axis=1)


def _attn_b_kernel(sink_ref, q_ref, kp_ref, kc_ref, kn_ref, vp_ref, vc_ref, vn_ref,
                   km_ref, vm_ref, o_ref, *, seq, heads, group):
    i = pl.program_id(1)
    nk = META_PAD + 3 * BLOCK
    kk = jnp.concatenate([km_ref[...], kp_ref[...], kc_ref[...], kn_ref[...]], axis=0)
    vv = jnp.concatenate([vm_ref[...], vp_ref[...], vc_ref[...], vn_ref[...]], axis=0)
    kk_sw = _swap_halves(kk)
    vv_sw = _swap_halves(vv)

    col = lax.broadcasted_iota(jnp.int32, (BLOCK, nk), 1)
    row = lax.broadcasted_iota(jnp.int32, (BLOCK, nk), 0)
    is_meta = col < META_PAD
    kpos_rel = col - (META_PAD + BLOCK)
    dt = jnp.abs(row - kpos_rel)
    kabs = i * BLOCK + kpos_rel
    vis = (col < N_META) | ((dt <= WINDOW) & (kabs >= 0) & (kabs < seq))
    negd = jnp.where(is_meta, 0.0, -dt.astype(F32))
    negd = jnp.where(vis, negd, NEG)

    q = q_ref[...] * SCALE
    lane = lax.broadcasted_iota(jnp.int32, (BLOCK, LANES), 1)
    lo = lane < HEAD_DIM
    zero = jnp.zeros((BLOCK, LANES), BF16)

    def masked_q(hd):
        blk = q[:, (hd // 2) * LANES:(hd // 2 + 1) * LANES]
        return jnp.where(lo if hd % 2 == 0 else ~lo, blk, zero)

    same = [hd for hd in range(heads) if (hd % 2) == (hd // group)]
    swap = [hd for hd in range(heads) if (hd % 2) != (hd // group)]

    def run(hds, keys, vals):
        qs = jnp.concatenate([masked_q(hd) for hd in hds], axis=0)
        s_all = lax.dot_general(qs, keys, _DN_NT, preferred_element_type=F32)
        ps, invs = [], []
        for r, hd in enumerate(hds):
            slope = 2.0 ** (-8.0 * (hd + 1) / heads)
            sink = sink_ref[hd]
            s = s_all[r * BLOCK:(r + 1) * BLOCK] + slope * negd
            m = jnp.maximum(jnp.max(s, axis=-1, keepdims=True), sink)
            p = jnp.exp(s - m)
            denom = jnp.sum(p, axis=-1, keepdims=True) + jnp.exp(sink - m)
            ps.append(p.astype(BF16))
            invs.append(1.0 / denom)
        out = jnp.dot(jnp.concatenate(ps, axis=0), vals, preferred_element_type=F32)
        return {hd: out[r * BLOCK:(r + 1) * BLOCK] * invs[r] for r, hd in enumerate(hds)}

    res = run(same, kk, vv)
    res.update(run(swap, kk_sw, vv_sw))
    blocks = []
    for j in range(heads // 2):
        blocks.append(jnp.where(lo, res[2 * j], res[2 * j + 1]))
    o_ref[...] = jnp.concatenate(blocks, axis=1).astype(BF16)


def _attn_b(pb, pb_meta, sink, batch, seq, heads, kv_heads):
    nb = seq // BLOCK
    qw = heads * HEAD_DIM
    kcol = qw // LANES
    vcol = kcol + 1
    assert kv_heads * HEAD_DIM == LANES
    blk = lambda off: (lambda b, i: (b * nb + jnp.clip(i + off, 0, nb - 1), 0))
    kv = lambda off, c: pl.BlockSpec(
        (BLOCK, LANES), lambda b, i: (b * nb + jnp.clip(i + off, 0, nb - 1), c))
    meta = lambda c: _resident((META_PAD, LANES), lambda b, i: (0, c))
    return pl.pallas_call(
        functools.partial(_attn_b_kernel, seq=seq, heads=heads, group=heads // kv_heads),
        grid=(batch, nb),
        in_specs=[
            pl.BlockSpec(memory_space=pltpu.SMEM),
            pl.BlockSpec((BLOCK, qw), blk(0)),
            kv(-1, kcol), kv(0, kcol), kv(1, kcol),
            kv(-1, vcol), kv(0, vcol), kv(1, vcol),
            meta(kcol), meta(vcol),
        ],
        out_specs=pl.BlockSpec((BLOCK, qw), blk(0)),
        out_shape=jax.ShapeDtypeStruct((batch * seq, qw), BF16),
        compiler_params=pltpu.CompilerParams(
            dimension_semantics=("arbitrary", "arbitrary"),
            vmem_limit_bytes=VMEM_LIMIT_BYTES),
        name="attn_b",
    )(sink, pb, pb, pb, pb, pb, pb, pb, pb_meta, pb_meta)


def _rms(x, g):
    ms = jnp.mean(x * x, axis=-1, keepdims=True)
    return (x * lax.rsqrt(ms + EPS)) * g


def _post_kernel(x_ref, ya_ref, yb_ref, gate_ref, wba_ref, wbb_ref, wo_ref, nf_ref,
                 wg_ref, wu_ref, wd_ref, nfin_ref, o_ref, *, d_model):
    dot = functools.partial(jnp.dot, preferred_element_type=F32)
    ga = gate_ref[:, :d_model].astype(F32)
    gb = gate_ref[:, d_model:].astype(F32)
    merged = ga * dot(ya_ref[...], wba_ref[...]) + gb * dot(yb_ref[...], wbb_ref[...])
    h1 = x_ref[...] + dot(merged.astype(BF16), wo_ref[...])
    hn = _rms(h1, nf_ref[...]).astype(BF16)
    act = jax.nn.silu(dot(hn, wg_ref[...])) * dot(hn, wu_ref[...])
    h2 = h1 + dot(act.astype(BF16), wd_ref[...])
    o_ref[...] = _rms(h2, nfin_ref[...])


def _post(x2d, ya, yb, gates, wba, wbb, wo, nf, wg, wu, wd, nfin, tm):
    m, d = x2d.shape
    row = lambda w: pl.BlockSpec((tm, w), lambda i: (i, 0))
    full = lambda a: _resident(a.shape, lambda i: (0, 0))
    return pl.pallas_call(
        functools.partial(_post_kernel, d_model=d),
        grid=(m // tm,),
        in_specs=[row(d), row(ya.shape[1]), row(yb.shape[1]), row(2 * d),
                  full(wba), full(wbb), full(wo), full(nf),
                  full(wg), full(wu), full(wd), full(nfin)],
        out_specs=row(d),
        out_shape=jax.ShapeDtypeStruct((m, d), F32),
        compiler_params=pltpu.CompilerParams(
            dimension_semantics=("arbitrary",), vmem_limit_bytes=VMEM_LIMIT_BYTES),
        name="post_ffn",
    )(x2d, ya, yb, gates, wba, wbb, wo, nf, wg, wu, wd, nfin)


def kernel(x, meta_tokens, norm_mix, w_in, lambda_q1, lambda_k1, lambda_q2, lambda_k2,
           subln_gain, sink_logits, w_branch_a, w_branch_b, w_out, norm_ffn,
           w_ff_gate, w_ff_up, w_ff_down, norm_final):
    batch, seq, d = x.shape
    assert norm_mix.shape[0] == 1, "single-layer block"
    a_heads = w_branch_a.shape[1] // A_V_DIM
    b_heads = sink_logits.shape[1]
    b_kv = max(1, b_heads // 4)
    qk_w = 2 * a_heads * 2 * HEAD_DIM
    va_w = a_heads * A_V_DIM
    pb_w = b_heads * HEAD_DIM + 2 * b_kv * HEAD_DIM
    pb_lo = qk_w + va_w
    gates_from = pb_lo + pb_w

    x2d = x.reshape(batch * seq, d)
    w_in_b = w_in[0].astype(BF16)
    wvt_b = w_in_b[:, qk_w:pb_lo].T
    gain = norm_mix[0][None, :]

    pa, pb, vt, gates = _in_proj(x2d, gain, w_in_b, wvt_b, ((0, qk_w), (pb_lo, gates_from)),
                                 gates_from, tm=A_KEY_CHUNK)
    pa_m, va_m, pb_m = _in_proj(meta_tokens.astype(F32), gain, w_in_b, wvt_b,
                                ((0, qk_w), (qk_w, pb_lo), (pb_lo, gates_from)), None, tm=N_META)
    pad = ((0, META_PAD - N_META), (0, 0))
    pa_m = jnp.pad(pa_m, pad)
    pb_m = jnp.pad(pb_m, pad)
    vt_m = jnp.pad(va_m, pad).T

    slopes = jnp.asarray([2.0 ** (-8.0 * (i + 1) / a_heads) for i in range(a_heads)], F32)
    ya = _attn_a(pa, vt, pa_m, vt_m, slopes, lambda_q1, lambda_k1, lambda_q2, lambda_k2,
                 subln_gain[0][:, None], batch, seq, a_heads,
                 tq=A_QUERY_TILE, tk=A_KEY_CHUNK)
    yb = _attn_b(pb, pb_m, sink_logits[0], batch, seq, b_heads, b_kv)

    out = _post(x2d, ya, yb, gates,
                w_branch_a[0].astype(BF16), w_branch_b[0].astype(BF16), w_out[0].astype(BF16),
                norm_ffn[0][None, :], w_ff_gate[0].astype(BF16), w_ff_up[0].astype(BF16),
                w_ff_down[0].astype(BF16), norm_final[None, :], tm=POST_ROW_TILE)
    return out.reshape(batch, seq, d)
```

```python
import functools
import math

import ml_dtypes
import numpy as np

import jax
import jax.numpy as jnp
from jax import lax
from jax.experimental import pallas as pl
from jax.experimental.pallas import tpu as pltpu

F32 = jnp.float32
BF16 = jnp.bfloat16

N_META = 16
HEAD_DIM = 64
WINDOW = 128
BLOCK = 128
A_V_DIM = 2 * HEAD_DIM
EPS = 1e-6
LAMBDA_INIT = 0.8 - 0.6 * math.exp(-0.3 * 0)
SCALE = HEAD_DIM ** -0.5
LOG2E = math.log2(math.e)

LANES = 128
BF16_SUBLANES = 16
META_PAD = 128
NEG = -1e30
VMEM_LIMIT_BYTES = 56 * 1024 * 1024

A_KEY_CHUNK = 512
A_QUERY_TILE = 512
POST_ROW_TILE = 256

A_VT_ROWS = A_V_DIM + BF16_SUBLANES

_DN_NT = (((1,), (1,)), ((), ()))


def _resident(block_shape, index_map):
    return pl.BlockSpec(block_shape, index_map, pipeline_mode=pl.Buffered(1))


def _in_proj_kernel(x_ref, g_ref, w_ref, wvt_ref, *out_refs, splits, gates_from):
    x = x_ref[...]
    ms = jnp.mean(x * x, axis=-1, keepdims=True)
    hn = ((x * lax.rsqrt(ms + EPS)) * g_ref[...]).astype(BF16)
    for (lo, hi), o_ref in zip(splits, out_refs):
        o_ref[...] = jnp.dot(hn, w_ref[:, lo:hi], preferred_element_type=F32).astype(BF16)
    if gates_from is not None:
        vt_ref, gate_ref = out_refs[len(splits):]
        vt = lax.dot_general(wvt_ref[...], hn, _DN_NT, preferred_element_type=F32)
        row = lax.broadcasted_iota(jnp.int32, vt.shape, 0)
        ones_row = functools.reduce(
            jnp.logical_or,
            [row == r for r in range(A_V_DIM, vt.shape[0], A_VT_ROWS)])
        vt_ref[0] = jnp.where(ones_row, 1.0, vt).astype(BF16)
        g = jnp.dot(hn, w_ref[:, gates_from:], preferred_element_type=F32)
        gate_ref[...] = jax.nn.sigmoid(g).astype(BF16)


def _in_proj(x2d, gain, w_bf16, wvt_bf16, splits, gates_from, tm):
    m, d = x2d.shape
    n_all = w_bf16.shape[1]
    nv = wvt_bf16.shape[0]
    widths = [hi - lo for lo, hi in splits]
    out_shape = [jax.ShapeDtypeStruct((m, w), BF16) for w in widths]
    out_specs = [pl.BlockSpec((tm, w), lambda i: (i, 0)) for w in widths]
    if gates_from is not None:
        out_shape += [jax.ShapeDtypeStruct((m // tm, nv, tm), BF16),
                      jax.ShapeDtypeStruct((m, n_all - gates_from), BF16)]
        out_specs += [pl.BlockSpec((1, nv, tm), lambda i: (i, 0, 0)),
                      pl.BlockSpec((tm, n_all - gates_from), lambda i: (i, 0))]
    return pl.pallas_call(
        functools.partial(_in_proj_kernel, splits=splits, gates_from=gates_from),
        grid=(m // tm,),
        in_specs=[
            pl.BlockSpec((tm, d), lambda i: (i, 0)),
            _resident((1, d), lambda i: (0, 0)),
            _resident((d, n_all), lambda i: (0, 0)),
            _resident((nv, d), lambda i: (0, 0)),
        ],
        out_specs=out_specs,
        out_shape=out_shape,
        compiler_params=pltpu.CompilerParams(
            dimension_semantics=("arbitrary",), vmem_limit_bytes=VMEM_LIMIT_BYTES),
        name="in_proj_meta" if gates_from is None else "in_proj_gates",
    )(x2d, gain, w_bf16, wvt_bf16)


_FEAT_SPLIT_BITS = 6
_N_PIECES = 3


def _bf16_pieces(x, n):
    pieces, rest = [], np.float32(x)
    for _ in range(n):
        piece = np.float32(np.asarray(rest, dtype=ml_dtypes.bfloat16))
        pieces.append(piece)
        rest = np.float32(rest - piece)
    return pieces


def _feature_table():
    e = _bf16_pieces(LOG2E, _N_PIECES)
    t = np.zeros((8, LANES), np.float32)
    n = _N_PIECES
    t[0, 0:n] = 1.0
    t[1, n:2 * n] = 1.0
    t[2, 2 * n:3 * n] = e
    t[2, 3 * n:4 * n] = e
    t[3, 2 * n:3 * n] = 1.0
    t[4, 3 * n:4 * n] = 1.0
    t[5, 0:n] = e
    t[5, n:2 * n] = e
    return jnp.asarray(t)


def _pos_terms(pos, slope):
    hi = (pos >> _FEAT_SPLIT_BITS).astype(F32) * (slope * (1 << _FEAT_SPLIT_BITS))
    lo = (pos & ((1 << _FEAT_SPLIT_BITS) - 1)).astype(F32) * slope
    return hi, lo


def _attn_a_kernel(slopes_ref, lq1_ref, lk1_ref, lq2_ref, lk2_ref, gain_ref, feat_ref,
                   q_ref, k_ref, vt_ref, km_ref, vtm_ref, o_ref,
                   kaug_ref, qfeat_ref, negabs_ref, *, tq, tk, seq):
    b = pl.program_id(0)
    h = pl.program_id(1)
    qi = pl.program_id(2)
    slope = slopes_ref[h]
    nqc = 2 * tq
    n_chunks = seq // tk
    n_off = tk // tq

    @pl.when((b == 0) & (h == 0) & (qi == 0))
    def _build_distance_tiles():
        for r in range(n_off):
            d = ((lax.broadcasted_iota(jnp.int32, (tk, nqc), 1) & (tq - 1))
                 - lax.broadcasted_iota(jnp.int32, (tk, nqc), 0) + r * tq)
            negabs_ref[r] = -jnp.abs(d).astype(F32)

    @pl.when(qi == 0)
    def _build_features():
        def fill(c, carry):
            rows = pl.ds(pl.multiple_of(c * tk, tk), tk)
            pos = c * tk + lax.broadcasted_iota(jnp.int32, (tk, LANES), 0)
            hi, lo = _pos_terms(pos, slope)
            kaug_ref[rows, :LANES] = k_ref[rows, :]
            kf = hi * feat_ref[0:1, :] + lo * feat_ref[1:2, :] + feat_ref[2:3, :]
            kaug_ref[rows, LANES:] = kf.astype(BF16)
            qf = feat_ref[5:6, :] - hi * feat_ref[3:4, :] - lo * feat_ref[4:5, :]
            qfeat_ref[rows, :] = qf.astype(BF16)
            return carry
        lax.fori_loop(0, n_chunks, fill, 0)

    q = q_ref[...]
    lane = lax.broadcasted_iota(jnp.int32, (tq, LANES), 1)
    zero = jnp.zeros_like(q)
    qq = jnp.concatenate([jnp.where(lane < HEAD_DIM, q, zero),
                          jnp.where(lane >= HEAD_DIM, q, zero)], axis=0)
    qf = qfeat_ref[pl.ds(pl.multiple_of(qi * tq, tq), tq), :]
    qf = jnp.concatenate([qf, qf], axis=0)
    q_left = jnp.concatenate([qq, qf], axis=1)
    q_right = jnp.concatenate([qq, -qf], axis=1)
    q_plain = jnp.concatenate([qq, jnp.zeros_like(qf)], axis=1)

    def scores(k_rows, q_aug):
        return lax.dot_general(k_rows, q_aug, _DN_NT, preferred_element_type=F32)

    km_aug = jnp.concatenate([km_ref[...], jnp.zeros((META_PAD, LANES), BF16)], axis=1)
    meta_valid = lax.broadcasted_iota(jnp.int32, (META_PAD, nqc), 0) < N_META
    s = jnp.where(meta_valid, scores(km_aug, q_plain), NEG)
    m = jnp.max(s, axis=0, keepdims=True)
    acc = jnp.dot(vtm_ref[...], jnp.exp2(s - m).astype(BF16), preferred_element_type=F32)

    c_diag = (qi * tq) // tk

    def chunk_of(t):
        return c_diag if t == 0 else (t - 1) + ((t - 1) >= c_diag).astype(jnp.int32)

    def stage_scores(t):
        c = chunk_of(t)
        rows = pl.ds(pl.multiple_of(c * tk, tk), tk)
        if t == 0:
            bias = negabs_ref[qi - c_diag * n_off] * (slope * LOG2E)
            return scores(kaug_ref[rows, :], q_plain) + bias
        return scores(kaug_ref[rows, :], jnp.where(c < c_diag, q_left, q_right))

    def stage_softmax(s, m):
        m_new = jnp.maximum(m, jnp.max(s, axis=0, keepdims=True))
        return m_new, jnp.exp2(s - m_new).astype(BF16), jnp.exp2(m - m_new)

    def stage_pv(t, p, alpha, acc):
        return alpha * acc + jnp.dot(vt_ref[chunk_of(t)], p, preferred_element_type=F32)

    s_cur = stage_scores(0)
    s_next = stage_scores(1)
    m, p, alpha = stage_softmax(s_cur, m)
    for t in range(n_chunks):
        s_cur, s_next = s_next, (stage_scores(t + 2) if t + 2 < n_chunks else None)
        acc = stage_pv(t, p, alpha, acc)
        if s_cur is not None:
            m, p, alpha = stage_softmax(s_cur, m)

    lam = (jnp.exp(jnp.sum(lq1_ref[...] * lk1_ref[...], axis=-1, keepdims=True))
           - jnp.exp(jnp.sum(lq2_ref[...] * lk2_ref[...], axis=-1, keepdims=True))
           + LAMBDA_INIT)
    o = acc[:A_V_DIM] / acc[A_V_DIM:A_V_DIM + 1]
    y = o[:, :tq] - lam * o[:, tq:]
    ms = jnp.mean(y * y, axis=0, keepdims=True)
    y = ((y * lax.rsqrt(ms + EPS)) * gain_ref[...]) * (1.0 - LAMBDA_INIT)
    o_ref[...] = y.T.astype(BF16)


def _attn_a(pa, vt, pa_meta, vt_meta, slopes, lq1, lk1, lq2, lk2, gain_col,
            batch, seq, heads, tq, tk):
    nq = seq // tq
    width = heads * A_V_DIM
    assert tk % tq == 0 and seq % tk == 0
    vec = lambda n: _resident((1, n), lambda b, h, i: (0, 0))
    return pl.pallas_call(
        functools.partial(_attn_a_kernel, tq=tq, tk=tk, seq=seq),
        grid=(batch, heads, nq),
        in_specs=[
            pl.BlockSpec(memory_space=pltpu.SMEM),
            vec(HEAD_DIM), vec(HEAD_DIM), vec(HEAD_DIM), vec(HEAD_DIM),
            _resident((A_V_DIM, 1), lambda b, h, i: (0, 0)),
            _resident((8, LANES), lambda b, h, i: (0, 0)),
            pl.BlockSpec((tq, LANES), lambda b, h, i: (b * nq + i, h)),
            pl.BlockSpec((seq, LANES), lambda b, h, i: (b, heads + h)),
            pl.BlockSpec((seq // tk, A_VT_ROWS, tk), lambda b, h, i: (b, h, 0)),
            pl.BlockSpec((META_PAD, LANES), lambda b, h, i: (0, heads + h)),
            pl.BlockSpec((A_VT_ROWS, META_PAD), lambda b, h, i: (h, 0)),
        ],
        out_specs=pl.BlockSpec((tq, A_V_DIM), lambda b, h, i: (b * nq + i, h)),
        out_shape=jax.ShapeDtypeStruct((batch * seq, width), BF16),
        scratch_shapes=[pltpu.VMEM((seq, 2 * LANES), BF16),
                        pltpu.VMEM((seq, LANES), BF16),
                        pltpu.VMEM((tk // tq, tk, 2 * tq), F32)],
        compiler_params=pltpu.CompilerParams(
            dimension_semantics=("arbitrary", "arbitrary", "arbitrary"),
            vmem_limit_bytes=VMEM_LIMIT_BYTES),
        name="attn_a",
    )(slopes, lq1, lk1, lq2, lk2, gain_col, _feature_table(), pa, pa, vt, pa_meta, vt_meta)


def _swap_halves(x):
    return jnp.concatenate([x[:, HEAD_DIM:], x[:, :HEAD_DIM]], axis=1)


def _attn_b_kernel(sink_ref, q_ref, kp_ref, kc_ref, kn_ref, vp_ref, vc_ref, vn_ref,
                   km_ref, vm_ref, o_ref, *, seq, heads, group):
    i = pl.program_id(1)
    nk = META_PAD + 3 * BLOCK
    kk = jnp.concatenate([km_ref[...], kp_ref[...], kc_ref[...], kn_ref[...]], axis=0)
    vv = jnp.concatenate([vm_ref[...], vp_ref[...], vc_ref[...], vn_ref[...]], axis=0)
    kk_sw = _swap_halves(kk)
    vv_sw = _swap_halves(vv)

    col = lax.broadcasted_iota(jnp.int32, (BLOCK, nk), 1)
    row = lax.broadcasted_iota(jnp.int32, (BLOCK, nk), 0)
    is_meta = col < META_PAD
    kpos_rel = col - (META_PAD + BLOCK)
    dt = jnp.abs(row - kpos_rel)
    kabs = i * BLOCK + kpos_rel
    vis = (col < N_META) | ((dt <= WINDOW) & (kabs >= 0) & (kabs < seq))
    negd = jnp.where(is_meta, 0.0, -dt.astype(F32))
    negd = jnp.where(vis, negd, NEG)

    q = q_ref[...] * SCALE
    lane = lax.broadcasted_iota(jnp.int32, (BLOCK, LANES), 1)
    lo = lane < HEAD_DIM
    zero = jnp.zeros((BLOCK, LANES), BF16)

    def masked_q(hd):
        blk = q[:, (hd // 2) * LANES:(hd // 2 + 1) * LANES]
        return jnp.where(lo if hd % 2 == 0 else ~lo, blk, zero)

    same = [hd for hd in range(heads) if (hd % 2) == (hd // group)]
    swap = [hd for hd in range(heads) if (hd % 2) != (hd // group)]

    def run(hds, keys, vals):
        qs = jnp.concatenate([masked_q(hd) for hd in hds], axis=0)
        s_all = lax.dot_general(qs, keys, _DN_NT, preferred_element_type=F32)
        ps, invs = [], []
        for r, hd in enumerate(hds):
            slope = 2.0 ** (-8.0 * (hd + 1) / heads)
            sink = sink_ref[hd]
            s = s_all[r * BLOCK:(r + 1) * BLOCK] + slope * negd
            m = jnp.maximum(jnp.max(s, axis=-1, keepdims=True), sink)
            p = jnp.exp(s - m)
            denom = jnp.sum(p, axis=-1, keepdims=True) + jnp.exp(sink - m)
            ps.append(p.astype(BF16))
            invs.append(1.0 / denom)
        out = jnp.dot(jnp.concatenate(ps, axis=0), vals, preferred_element_type=F32)
        return {hd: out[r * BLOCK:(r + 1) * BLOCK] * invs[r] for r, hd in enumerate(hds)}

    res = run(same, kk, vv)
    res.update(run(swap, kk_sw, vv_sw))
    blocks = []
    for j in range(heads // 2):
        blocks.append(jnp.where(lo, res[2 * j], res[2 * j + 1]))
    o_ref[...] = jnp.concatenate(blocks, axis=1).astype(BF16)


def _attn_b(pb, pb_meta, sink, batch, seq, heads, kv_heads):
    nb = seq // BLOCK
    qw = heads * HEAD_DIM
    kcol = qw // LANES
    vcol = kcol + 1
    assert kv_heads * HEAD_DIM == LANES
    blk = lambda off: (lambda b, i: (b * nb + jnp.clip(i + off, 0, nb - 1), 0))
    kv = lambda off, c: pl.BlockSpec(
        (BLOCK, LANES), lambda b, i: (b * nb + jnp.clip(i + off, 0, nb - 1), c))
    meta = lambda c: _resident((META_PAD, LANES), lambda b, i: (0, c))
    return pl.pallas_call(
        functools.partial(_attn_b_kernel, seq=seq, heads=heads, group=heads // kv_heads),
        grid=(batch, nb),
        in_specs=[
            pl.BlockSpec(memory_space=pltpu.SMEM),
            pl.BlockSpec((BLOCK, qw), blk(0)),
            kv(-1, kcol), kv(0, kcol), kv(1, kcol),
            kv(-1, vcol), kv(0, vcol), kv(1, vcol),
            meta(kcol), meta(vcol),
        ],
        out_specs=pl.BlockSpec((BLOCK, qw), blk(0)),
        out_shape=jax.ShapeDtypeStruct((batch * seq, qw), BF16),
        compiler_params=pltpu.CompilerParams(
            dimension_semantics=("arbitrary", "arbitrary"),
            vmem_limit_bytes=VMEM_LIMIT_BYTES),
        name="attn_b",
    )(sink, pb, pb, pb, pb, pb, pb, pb, pb_meta, pb_meta)


def _rms(x, g):
    ms = jnp.mean(x * x, axis=-1, keepdims=True)
    return (x * lax.rsqrt(ms + EPS)) * g


def _post_kernel(x_ref, ya_ref, yb_ref, gate_ref, wba_ref, wbb_ref, wo_ref, nf_ref,
                 wg_ref, wu_ref, wd_ref, nfin_ref, o_ref, *, d_model):
    dot = functools.partial(jnp.dot, preferred_element_type=F32)
    ga = gate_ref[:, :d_model].astype(F32)
    gb = gate_ref[:, d_model:].astype(F32)
    merged = ga * dot(ya_ref[...], wba_ref[...]) + gb * dot(yb_ref[...], wbb_ref[...])
    h1 = x_ref[...] + dot(merged.astype(BF16), wo_ref[...])
    hn = _rms(h1, nf_ref[...]).astype(BF16)
    act = jax.nn.silu(dot(hn, wg_ref[...])) * dot(hn, wu_ref[...])
    h2 = h1 + dot(act.astype(BF16), wd_ref[...])
    o_ref[...] = _rms(h2, nfin_ref[...])


def _post(x2d, ya, yb, gates, wba, wbb, wo, nf, wg, wu, wd, nfin, tm):
    m, d = x2d.shape
    row = lambda w: pl.BlockSpec((tm, w), lambda i: (i, 0))
    full = lambda a: _resident(a.shape, lambda i: (0, 0))
    return pl.pallas_call(
        functools.partial(_post_kernel, d_model=d),
        grid=(m // tm,),
        in_specs=[row(d), row(ya.shape[1]), row(yb.shape[1]), row(2 * d),
                  full(wba), full(wbb), full(wo), full(nf),
                  full(wg), full(wu), full(wd), full(nfin)],
        out_specs=row(d),
        out_shape=jax.ShapeDtypeStruct((m, d), F32),
        compiler_params=pltpu.CompilerParams(
            dimension_semantics=("arbitrary",), vmem_limit_bytes=VMEM_LIMIT_BYTES),
        name="post_ffn",
    )(x2d, ya, yb, gates, wba, wbb, wo, nf, wg, wu, wd, nfin)


def _extend_heads(vt_heads, first_row):
    heads, _, n = vt_heads.shape
    extra = jnp.zeros((heads, BF16_SUBLANES, n), vt_heads.dtype).at[:, 0, :].set(first_row)
    return jnp.concatenate([vt_heads, extra], axis=1).reshape(heads * A_VT_ROWS, n)


def kernel(x, meta_tokens, norm_mix, w_in, lambda_q1, lambda_k1, lambda_q2, lambda_k2,
           subln_gain, sink_logits, w_branch_a, w_branch_b, w_out, norm_ffn,
           w_ff_gate, w_ff_up, w_ff_down, norm_final):
    batch, seq, d = x.shape
    assert norm_mix.shape[0] == 1, "single-layer block"
    a_heads = w_branch_a.shape[1] // A_V_DIM
    b_heads = sink_logits.shape[1]
    b_kv = max(1, b_heads // 4)
    qa_w = a_heads * 2 * HEAD_DIM
    qk_w = 2 * qa_w
    va_w = a_heads * A_V_DIM
    pb_w = b_heads * HEAD_DIM + 2 * b_kv * HEAD_DIM
    pb_lo = qk_w + va_w
    gates_from = pb_lo + pb_w

    x2d = x.reshape(batch * seq, d)
    col_scale = jnp.ones((w_in.shape[2],), F32).at[:qa_w].set(SCALE * LOG2E)
    w_in_b = (w_in[0] * col_scale).astype(BF16)
    wv = w_in_b[:, qk_w:pb_lo]
    wvt_b = _extend_heads(wv.T.reshape(a_heads, A_V_DIM, d), 0)
    gain = norm_mix[0][None, :]

    pa, pb, vt, gates = _in_proj(x2d, gain, w_in_b, wvt_b, ((0, qk_w), (pb_lo, gates_from)),
                                 gates_from, tm=A_KEY_CHUNK)
    pa_m, va_m, pb_m = _in_proj(meta_tokens.astype(F32), gain, w_in_b, wvt_b,
                                ((0, qk_w), (qk_w, pb_lo), (pb_lo, gates_from)), None, tm=N_META)
    pad = ((0, META_PAD - N_META), (0, 0))
    pa_m = jnp.pad(pa_m, pad)
    pb_m = jnp.pad(pb_m, pad)
    vt_m = _extend_heads(jnp.pad(va_m, pad).T.reshape(a_heads, A_V_DIM, META_PAD), 1)

    slopes = jnp.asarray([2.0 ** (-8.0 * (i + 1) / a_heads) for i in range(a_heads)], F32)
    ya = _attn_a(pa, vt, pa_m, vt_m, slopes, lambda_q1, lambda_k1, lambda_q2, lambda_k2,
                 subln_gain[0][:, None], batch, seq, a_heads,
                 tq=A_QUERY_TILE, tk=A_KEY_CHUNK)
    yb = _attn_b(pb, pb_m, sink_logits[0], batch, seq, b_heads, b_kv)

    out = _post(x2d, ya, yb, gates,
                w_branch_a[0].astype(BF16), w_branch_b[0].astype(BF16), w_out[0].astype(BF16),
                norm_ffn[0][None, :], w_ff_gate[0].astype(BF16), w_ff_up[0].astype(BF16),
                w_ff_down[0].astype(BF16), norm_final[None, :], tm=POST_ROW_TILE)
    return out.reshape(batch, seq, d)
```

```python
import functools
import math

import ml_dtypes
import numpy as np

import jax
import jax.numpy as jnp
from jax import lax
from jax.experimental import pallas as pl
from jax.experimental.pallas import tpu as pltpu

F32 = jnp.float32
BF16 = jnp.bfloat16

N_META = 16
HEAD_DIM = 64
WINDOW = 128
BLOCK = 128
A_V_DIM = 2 * HEAD_DIM
EPS = 1e-6
LAMBDA_INIT = 0.8 - 0.6 * math.exp(-0.3 * 0)
SCALE = HEAD_DIM ** -0.5
LOG2E = math.log2(math.e)

LANES = 128
BF16_SUBLANES = 16
META_PAD = 128
NEG = -1e30
VMEM_LIMIT_BYTES = 56 * 1024 * 1024

A_KEY_CHUNK = 512
A_QUERY_TILE = 512
A_SCORES_AHEAD = 2
A_STREAMS = 1
POST_ROW_TILE = 512
FFN_COLUMN_GROUP = 1536

A_VT_ROWS = A_V_DIM + BF16_SUBLANES
B_VT_ROWS = HEAD_DIM + BF16_SUBLANES
B_BLOCKS_PER_STEP = 2
_BLOCK_BITS = BLOCK.bit_length() - 1

_DN_NT = (((1,), (1,)), ((), ()))


def _resident(block_shape, index_map):
    return pl.BlockSpec(block_shape, index_map, pipeline_mode=pl.Buffered(1))


def _in_proj_kernel(x_ref, g_ref, w_ref, wvt_ref, *out_refs, splits, gates_from, vt_splits,
                    ones_rows):
    x = x_ref[...]
    ms = jnp.mean(x * x, axis=-1, keepdims=True)
    hn = ((x * lax.rsqrt(ms + EPS)) * g_ref[...]).astype(BF16)
    for (lo, hi), o_ref in zip(splits, out_refs):
        o_ref[...] = jnp.dot(hn, w_ref[:, lo:hi], preferred_element_type=F32).astype(BF16)
    if gates_from is not None:
        vt_refs = out_refs[len(splits):-1]
        vt = lax.dot_general(wvt_ref[...], hn, _DN_NT, preferred_element_type=F32)
        row = lax.broadcasted_iota(jnp.int32, vt.shape, 0)
        is_ones = functools.reduce(jnp.logical_or, [row == r for r in ones_rows])
        vt = jnp.where(is_ones, 1.0, vt).astype(BF16)
        for (lo, hi), vt_ref in zip(vt_splits, vt_refs):
            vt_ref[0] = vt[lo:hi]
        g = jnp.dot(hn, w_ref[:, gates_from:], preferred_element_type=F32)
        out_refs[-1][...] = jax.nn.sigmoid(g).astype(BF16)


def _in_proj(x2d, gain, w_bf16, wvt_bf16, splits, gates_from, vt_splits, ones_rows, tm):
    m, d = x2d.shape
    n_all = w_bf16.shape[1]
    nv = wvt_bf16.shape[0]
    widths = [hi - lo for lo, hi in splits]
    out_shape = [jax.ShapeDtypeStruct((m, w), BF16) for w in widths]
    out_specs = [pl.BlockSpec((tm, w), lambda i: (i, 0)) for w in widths]
    if gates_from is not None:
        for lo, hi in vt_splits:
            out_shape.append(jax.ShapeDtypeStruct((m // tm, hi - lo, tm), BF16))
            out_specs.append(pl.BlockSpec((1, hi - lo, tm), lambda i: (i, 0, 0)))
        out_shape.append(jax.ShapeDtypeStruct((m, n_all - gates_from), BF16))
        out_specs.append(pl.BlockSpec((tm, n_all - gates_from), lambda i: (i, 0)))
    return pl.pallas_call(
        functools.partial(_in_proj_kernel, splits=splits, gates_from=gates_from,
                          vt_splits=vt_splits, ones_rows=ones_rows),
        grid=(m // tm,),
        in_specs=[
            pl.BlockSpec((tm, d), lambda i: (i, 0)),
            _resident((1, d), lambda i: (0, 0)),
            _resident((d, n_all), lambda i: (0, 0)),
            _resident((nv, d), lambda i: (0, 0)),
        ],
        out_specs=out_specs,
        out_shape=out_shape,
        compiler_params=pltpu.CompilerParams(
            dimension_semantics=("arbitrary",), vmem_limit_bytes=VMEM_LIMIT_BYTES),
        name="in_proj_meta" if gates_from is None else "in_proj_gates",
    )(x2d, gain, w_bf16, wvt_bf16)


_FEAT_SPLIT_BITS = 6
_N_PIECES = 3


def _bf16_pieces(x, n):
    pieces, rest = [], np.float32(x)
    for _ in range(n):
        piece = np.float32(np.asarray(rest, dtype=ml_dtypes.bfloat16))
        pieces.append(piece)
        rest = np.float32(rest - piece)
    return pieces


def _feature_table():
    e = _bf16_pieces(LOG2E, _N_PIECES)
    t = np.zeros((8, LANES), np.float32)
    n = _N_PIECES
    t[0, 0:n] = 1.0
    t[1, n:2 * n] = 1.0
    t[2, 2 * n:3 * n] = e
    t[2, 3 * n:4 * n] = e
    t[3, 2 * n:3 * n] = 1.0
    t[4, 3 * n:4 * n] = 1.0
    t[5, 0:n] = e
    t[5, n:2 * n] = e
    return jnp.asarray(t)


def _pos_terms(pos, slope):
    hi = (pos >> _FEAT_SPLIT_BITS).astype(F32) * (slope * (1 << _FEAT_SPLIT_BITS))
    lo = (pos & ((1 << _FEAT_SPLIT_BITS) - 1)).astype(F32) * slope
    return hi, lo


def _attn_a_kernel(slopes_ref, lq1_ref, lk1_ref, lq2_ref, lk2_ref, gain_ref, feat_ref,
                   q_ref, k_ref, vt_ref, km_ref, vtm_ref, o_ref,
                   kaug_ref, qfeat_ref, negabs_ref, *, tq, tk, seq, n_streams):
    b = pl.program_id(0)
    h = pl.program_id(1)
    qi = pl.program_id(2)
    slope = slopes_ref[h]
    ts = tq // n_streams
    nqc = 2 * ts
    n_chunks = seq // tk
    n_off = tk // ts

    @pl.when((b == 0) & (h == 0) & (qi == 0))
    def _build_distance_tiles():
        for r in range(n_off):
            d = ((lax.broadcasted_iota(jnp.int32, (tk, nqc), 1) & (ts - 1))
                 - lax.broadcasted_iota(jnp.int32, (tk, nqc), 0) + r * ts)
            negabs_ref[r] = -jnp.abs(d).astype(F32)

    @pl.when(qi == 0)
    def _build_features():
        def fill(c, carry):
            rows = pl.ds(pl.multiple_of(c * tk, tk), tk)
            pos = c * tk + lax.broadcasted_iota(jnp.int32, (tk, LANES), 0)
            hi, lo = _pos_terms(pos, slope)
            kaug_ref[rows, :LANES] = k_ref[rows, :]
            kf = hi * feat_ref[0:1, :] + lo * feat_ref[1:2, :] + feat_ref[2:3, :]
            kaug_ref[rows, LANES:] = kf.astype(BF16)
            qf = feat_ref[5:6, :] - hi * feat_ref[3:4, :] - lo * feat_ref[4:5, :]
            qfeat_ref[rows, :] = qf.astype(BF16)
            return carry
        lax.fori_loop(0, n_chunks, fill, 0)

    def scores(k_rows, q_aug):
        return lax.dot_general(k_rows, q_aug, _DN_NT, preferred_element_type=F32)

    lam = (jnp.exp(jnp.sum(lq1_ref[...] * lk1_ref[...], axis=-1, keepdims=True))
           - jnp.exp(jnp.sum(lq2_ref[...] * lk2_ref[...], axis=-1, keepdims=True))
           + LAMBDA_INIT)
    km_aug = jnp.concatenate([km_ref[...], jnp.zeros((META_PAD, LANES), BF16)], axis=1)
    meta_valid = lax.broadcasted_iota(jnp.int32, (META_PAD, nqc), 0) < N_META
    lane = lax.broadcasted_iota(jnp.int32, (ts, LANES), 1)

    class Stream:
        def __init__(self, u):
            self.u = u
            self.qs = qi * n_streams + u
            self.c_diag = (self.qs * ts) // tk
            q = q_ref[u * ts:(u + 1) * ts, :]
            zero = jnp.zeros_like(q)
            qq = jnp.concatenate([jnp.where(lane < HEAD_DIM, q, zero),
                                  jnp.where(lane >= HEAD_DIM, q, zero)], axis=0)
            qf = qfeat_ref[pl.ds(pl.multiple_of(self.qs * ts, ts), ts), :]
            qf = jnp.concatenate([qf, qf], axis=0)
            self.q_left = jnp.concatenate([qq, qf], axis=1)
            self.q_right = jnp.concatenate([qq, -qf], axis=1)
            self.q_plain = jnp.concatenate([qq, jnp.zeros_like(qf)], axis=1)
            s = jnp.where(meta_valid, scores(km_aug, self.q_plain), NEG)
            self.m = jnp.max(s, axis=0, keepdims=True)
            self.acc = jnp.dot(vtm_ref[...], jnp.exp2(s - self.m).astype(BF16),
                               preferred_element_type=F32)
            self.pending = []

        def chunk_of(self, t):
            return self.c_diag if t == 0 else (t - 1) + ((t - 1) >= self.c_diag).astype(jnp.int32)

        def issue_scores(self, t):
            c = self.chunk_of(t)
            rows = pl.ds(pl.multiple_of(c * tk, tk), tk)
            if t == 0:
                bias = negabs_ref[self.qs - self.c_diag * n_off] * (slope * LOG2E)
                s = scores(kaug_ref[rows, :], self.q_plain) + bias
            else:
                s = scores(kaug_ref[rows, :], jnp.where(c < self.c_diag, self.q_left, self.q_right))
            self.pending.append(s)

        def softmax(self):
            s = self.pending.pop(0)
            m_new = jnp.maximum(self.m, jnp.max(s, axis=0, keepdims=True))
            self.p = jnp.exp2(s - m_new).astype(BF16)
            self.alpha = jnp.exp2(self.m - m_new)
            self.m = m_new

        def pv(self, t):
            self.acc = self.alpha * self.acc + jnp.dot(vt_ref[self.chunk_of(t)], self.p,
                                                       preferred_element_type=F32)

        def finish(self):
            o = self.acc[:A_V_DIM] / self.acc[A_V_DIM:A_V_DIM + 1]
            y = o[:, :ts] - lam * o[:, ts:]
            ms = jnp.mean(y * y, axis=0, keepdims=True)
            y = ((y * lax.rsqrt(ms + EPS)) * gain_ref[...]) * (1.0 - LAMBDA_INIT)
            o_ref[self.u * ts:(self.u + 1) * ts, :] = y.T.astype(BF16)

    for u in range(n_streams):
        st = Stream(u)
        for t in range(min(A_SCORES_AHEAD, n_chunks)):
            st.issue_scores(t)
        st.softmax()
        for t in range(n_chunks):
            if t + A_SCORES_AHEAD < n_chunks:
                st.issue_scores(t + A_SCORES_AHEAD)
            st.pv(t)
            if st.pending:
                st.softmax()
        st.finish()


def _attn_a(pa, vt, pa_meta, vt_meta, slopes, lq1, lk1, lq2, lk2, gain_col,
            batch, seq, heads, tq, tk):
    nq = seq // tq
    width = heads * A_V_DIM
    ts = tq // A_STREAMS
    assert tq % A_STREAMS == 0 and tk % ts == 0 and seq % tk == 0 and seq % tq == 0
    vec = lambda n: _resident((1, n), lambda b, h, i: (0, 0))
    return pl.pallas_call(
        functools.partial(_attn_a_kernel, tq=tq, tk=tk, seq=seq, n_streams=A_STREAMS),
        grid=(batch, heads, nq),
        in_specs=[
            pl.BlockSpec(memory_space=pltpu.SMEM),
            vec(HEAD_DIM), vec(HEAD_DIM), vec(HEAD_DIM), vec(HEAD_DIM),
            _resident((A_V_DIM, 1), lambda b, h, i: (0, 0)),
            _resident((8, LANES), lambda b, h, i: (0, 0)),
            pl.BlockSpec((tq, LANES), lambda b, h, i: (b * nq + i, h)),
            pl.BlockSpec((seq, LANES), lambda b, h, i: (b, heads + h)),
            pl.BlockSpec((seq // tk, A_VT_ROWS, tk), lambda b, h, i: (b, h, 0)),
            pl.BlockSpec((META_PAD, LANES), lambda b, h, i: (0, heads + h)),
            pl.BlockSpec((A_VT_ROWS, META_PAD), lambda b, h, i: (h, 0)),
        ],
        out_specs=pl.BlockSpec((tq, A_V_DIM), lambda b, h, i: (b * nq + i, h)),
        out_shape=jax.ShapeDtypeStruct((batch * seq, width), BF16),
        scratch_shapes=[pltpu.VMEM((seq, 2 * LANES), BF16),
                        pltpu.VMEM((seq, LANES), BF16),
                        pltpu.VMEM((tk // ts, tk, 2 * ts), F32)],
        compiler_params=pltpu.CompilerParams(
            dimension_semantics=("arbitrary", "arbitrary", "arbitrary"),
            vmem_limit_bytes=VMEM_LIMIT_BYTES),
        name="attn_a",
    )(slopes, lq1, lk1, lq2, lk2, gain_col, _feature_table(), pa, pa, vt, pa_meta, vt_meta)


def _attn_b_kernel(sink_ref, q_ref, kprev_ref, kcur_ref, knext_ref, vprev_ref, vcur_ref,
                   vnext_ref, km_ref, vtm_ref, o_ref, bias_ref, *, seq, heads, group, nblk):
    b = pl.program_id(0)
    step = pl.program_id(1)
    nb = seq // BLOCK
    kv_heads = heads // group
    nwin = 3 * BLOCK
    nkeys = nwin + N_META
    ncols = heads * BLOCK
    gcols = group * BLOCK
    k_pad = -(-nkeys // (2 * LANES)) * (2 * LANES)

    @pl.when((b == 0) & (step == 0))
    def _build_bias():
        kr = lax.broadcasted_iota(jnp.int32, (nkeys, ncols), 0)
        col = lax.broadcasted_iota(jnp.int32, (nkeys, ncols), 1)
        dt = jnp.abs((col & (BLOCK - 1)) - (kr - BLOCK))
        head = (col >> _BLOCK_BITS).astype(F32)
        slope = jnp.exp2(-(head + 1.0) * (8.0 / heads))
        base = jnp.where((kr < nwin) & (dt <= WINDOW), dt.astype(F32) * (-LOG2E) * slope,
                         jnp.where(kr >= nwin, 0.0, NEG))
        bias_ref[0] = base
        bias_ref[1] = jnp.where(kr < BLOCK, NEG, base)
        bias_ref[2] = jnp.where((kr >= 2 * BLOCK) & (kr < nwin), NEG, base)

    col_head = lax.broadcasted_iota(jnp.int32, (1, ncols), 1) >> _BLOCK_BITS
    sink_row = jnp.zeros((1, ncols), F32)
    for hd in range(heads):
        sink_row = jnp.where(col_head == hd, sink_ref[hd] * LOG2E, sink_row)

    lane = lax.broadcasted_iota(jnp.int32, (BLOCK, LANES), 1)
    lo = lane < HEAD_DIM
    zero = jnp.zeros((BLOCK, LANES), BF16)
    k_blocks = [kprev_ref[...], kcur_ref[:BLOCK, :], kcur_ref[BLOCK:, :], knext_ref[...]]
    vt_blocks = [vprev_ref[0], vcur_ref[0, :, :BLOCK], vcur_ref[0, :, BLOCK:], vnext_ref[0]]
    k_meta = km_ref[:N_META, :]
    vt_meta = vtm_ref[...]

    all_scores = []
    for blk in range(nblk):
        i = step * nblk + blk
        k_win = jnp.concatenate(k_blocks[blk:blk + 3] + [k_meta], axis=0)

        q = q_ref[blk * BLOCK:(blk + 1) * BLOCK, :]
        q_heads = []
        for hd in range(heads):
            part = q[:, (hd // 2) * LANES:(hd // 2 + 1) * LANES]
            in_lo, want_lo = hd % 2 == 0, (hd // group) % 2 == 0
            if in_lo != want_lo:
                part = jnp.concatenate([part[:, HEAD_DIM:], part[:, :HEAD_DIM]], axis=1)
            q_heads.append(jnp.where(lo if want_lo else ~lo, part, zero))
        q_all = jnp.concatenate(q_heads, axis=0)

        variant = jnp.where(i == 0, 1, jnp.where(i == nb - 1, 2, 0))
        all_scores.append(
            lax.dot_general(k_win, q_all, _DN_NT, preferred_element_type=F32) + bias_ref[variant])

    for blk, s in enumerate(all_scores):
        vt_win = jnp.concatenate(vt_blocks[blk:blk + 3] + [vt_meta], axis=1)
        m = jnp.maximum(jnp.max(s, axis=0, keepdims=True), sink_row)
        p = jnp.exp2(s - m).astype(BF16)
        p = jnp.concatenate([p, jnp.zeros((k_pad - nkeys, ncols), BF16)], axis=0)
        sink_term = jnp.exp2(sink_row - m)

        outs = []
        for g in range(kv_heads):
            cols = slice(g * gcols, (g + 1) * gcols)
            og = jnp.dot(vt_win[g * B_VT_ROWS:(g + 1) * B_VT_ROWS], p[:, cols],
                         preferred_element_type=F32)
            outs.append(og[:HEAD_DIM] / (og[HEAD_DIM:HEAD_DIM + 1] + sink_term[:, cols]))
        pairs = []
        for j in range(heads // 2):
            g, first = (2 * j) // group, (2 * j) % group
            pair = outs[g][:, first * BLOCK:(first + 2) * BLOCK]
            pairs.append(jnp.concatenate([pair[:, :BLOCK], pair[:, BLOCK:]], axis=0).T)
        o_ref[blk * BLOCK:(blk + 1) * BLOCK, :] = jnp.concatenate(pairs, axis=1).astype(BF16)


def _attn_b(pb, vtb, pb_meta, vtb_meta, sink, batch, seq, heads, kv_heads, nblk, slab):
    nb = seq // BLOCK
    n_steps = nb // nblk
    qw = heads * HEAD_DIM
    kcol = qw // LANES
    rows = kv_heads * B_VT_ROWS
    per_slab = slab // BLOCK
    assert kv_heads * HEAD_DIM == LANES and nblk == 2 and nb % nblk == 0 and per_slab % nblk == 0
    prev_blk = lambda i: jnp.maximum(nblk * i - 1, 0)
    next_blk = lambda i: jnp.minimum(nblk * i + nblk, nb - 1)
    vt_spec = lambda blk_of: pl.BlockSpec(
        (1, rows, BLOCK),
        lambda b, i: (b * (nb // per_slab) + blk_of(i) // per_slab, 0, blk_of(i) % per_slab))
    return pl.pallas_call(
        functools.partial(_attn_b_kernel, seq=seq, heads=heads, group=heads // kv_heads, nblk=nblk),
        grid=(batch, n_steps),
        in_specs=[
            pl.BlockSpec(memory_space=pltpu.SMEM),
            pl.BlockSpec((nblk * BLOCK, qw), lambda b, i: (b * n_steps + i, 0)),
            pl.BlockSpec((BLOCK, LANES), lambda b, i: (b * nb + prev_blk(i), kcol)),
            pl.BlockSpec((nblk * BLOCK, LANES), lambda b, i: (b * n_steps + i, kcol)),
            pl.BlockSpec((BLOCK, LANES), lambda b, i: (b * nb + next_blk(i), kcol)),
            vt_spec(prev_blk),
            pl.BlockSpec((1, rows, nblk * BLOCK),
                         lambda b, i: (b * (nb // per_slab) + (nblk * i) // per_slab, 0,
                                       i % (per_slab // nblk))),
            vt_spec(next_blk),
            _resident((META_PAD, LANES), lambda b, i: (0, kcol)),
            _resident((rows, META_PAD), lambda b, i: (0, 0)),
        ],
        out_specs=pl.BlockSpec((nblk * BLOCK, qw), lambda b, i: (b * n_steps + i, 0)),
        out_shape=jax.ShapeDtypeStruct((batch * seq, qw), BF16),
        scratch_shapes=[pltpu.VMEM((3, 3 * BLOCK + N_META, heads * BLOCK), F32)],
        compiler_params=pltpu.CompilerParams(
            dimension_semantics=("arbitrary", "arbitrary"),
            vmem_limit_bytes=VMEM_LIMIT_BYTES),
        name="attn_b",
    )(sink, pb, pb, pb, pb, vtb, vtb, vtb, pb_meta, vtb_meta)


def _rms(x, g):
    ms = jnp.mean(x * x, axis=-1, keepdims=True)
    return (x * lax.rsqrt(ms + EPS)) * g


def _post_kernel(x_ref, ya_ref, yb_ref, gate_ref, wba_ref, wbb_ref, wo_ref, nf_ref,
                 wg_ref, wu_ref, wd_ref, nfin_ref, o_ref, *, d_model):
    dot = functools.partial(jnp.dot, preferred_element_type=F32)
    ga = gate_ref[:, :d_model].astype(F32)
    gb = gate_ref[:, d_model:].astype(F32)
    merged = ga * dot(ya_ref[...], wba_ref[...]) + gb * dot(yb_ref[...], wbb_ref[...])
    h1 = x_ref[...] + dot(merged.astype(BF16), wo_ref[...])
    hn = _rms(h1, nf_ref[...]).astype(BF16)
    d_ff = wg_ref.shape[1]
    bounds = [min(k * FFN_COLUMN_GROUP, d_ff) for k in range(-(-d_ff // FFN_COLUMN_GROUP) + 1)]
    h2 = h1
    for lo, hi in zip(bounds[:-1], bounds[1:]):
        act = jax.nn.silu(dot(hn, wg_ref[:, lo:hi])) * dot(hn, wu_ref[:, lo:hi])
        h2 = h2 + dot(act.astype(BF16), wd_ref[lo:hi, :])
    o_ref[...] = _rms(h2, nfin_ref[...])


def _post(x2d, ya, yb, gates, wba, wbb, wo, nf, wg, wu, wd, nfin, tm):
    m, d = x2d.shape
    row = lambda w: pl.BlockSpec((tm, w), lambda i: (i, 0))
    full = lambda a: _resident(a.shape, lambda i: (0, 0))
    return pl.pallas_call(
        functools.partial(_post_kernel, d_model=d),
        grid=(m // tm,),
        in_specs=[row(d), row(ya.shape[1]), row(yb.shape[1]), row(2 * d),
                  full(wba), full(wbb), full(wo), full(nf),
                  full(wg), full(wu), full(wd), full(nfin)],
        out_specs=row(d),
        out_shape=jax.ShapeDtypeStruct((m, d), F32),
        compiler_params=pltpu.CompilerParams(
            dimension_semantics=("arbitrary",), vmem_limit_bytes=VMEM_LIMIT_BYTES),
        name="post_ffn",
    )(x2d, ya, yb, gates, wba, wbb, wo, nf, wg, wu, wd, nfin)


def _extend_heads(vt_heads, first_row):
    heads, dim, n = vt_heads.shape
    extra = jnp.zeros((heads, BF16_SUBLANES, n), vt_heads.dtype).at[:, 0, :].set(first_row)
    return jnp.concatenate([vt_heads, extra], axis=1).reshape(heads * (dim + BF16_SUBLANES), n)


def kernel(x, meta_tokens, norm_mix, w_in, lambda_q1, lambda_k1, lambda_q2, lambda_k2,
           subln_gain, sink_logits, w_branch_a, w_branch_b, w_out, norm_ffn,
           w_ff_gate, w_ff_up, w_ff_down, norm_final):
    batch, seq, d = x.shape
    assert norm_mix.shape[0] == 1, "single-layer block"
    a_heads = w_branch_a.shape[1] // A_V_DIM
    b_heads = sink_logits.shape[1]
    b_kv = max(1, b_heads // 4)
    qa_w = a_heads * 2 * HEAD_DIM
    qk_w = 2 * qa_w
    va_w = a_heads * A_V_DIM
    qb_w = b_heads * HEAD_DIM
    kvb_w = b_kv * HEAD_DIM
    pb_lo = qk_w + va_w
    vb_lo = pb_lo + qb_w + kvb_w
    gates_from = vb_lo + kvb_w

    x2d = x.reshape(batch * seq, d)
    col_scale = (jnp.ones((w_in.shape[2],), F32)
                 .at[:qa_w].set(SCALE * LOG2E).at[pb_lo:pb_lo + qb_w].set(SCALE * LOG2E))
    w_in_b = (w_in[0] * col_scale).astype(BF16)
    wvt_a = _extend_heads(w_in_b[:, qk_w:pb_lo].T.reshape(a_heads, A_V_DIM, d), 0)
    wvt_b = _extend_heads(w_in_b[:, vb_lo:gates_from].T.reshape(b_kv, HEAD_DIM, d), 0)
    wvt = jnp.concatenate([wvt_a, wvt_b], axis=0)
    na, nbt = wvt_a.shape[0], wvt_b.shape[0]
    ones_rows = (tuple(range(A_V_DIM, na, A_VT_ROWS))
                 + tuple(range(na + HEAD_DIM, na + nbt, B_VT_ROWS)))
    gain = norm_mix[0][None, :]

    pa, pb, vta, vtb, gates = _in_proj(
        x2d, gain, w_in_b, wvt, ((0, qk_w), (pb_lo, vb_lo)), gates_from,
        ((0, na), (na, na + nbt)), ones_rows, tm=A_KEY_CHUNK)
    pa_m, va_m, pb_m, vb_m = _in_proj(
        meta_tokens.astype(F32), gain, w_in_b, wvt,
        ((0, qk_w), (qk_w, pb_lo), (pb_lo, vb_lo), (vb_lo, gates_from)), None, (), (), tm=N_META)
    pad = ((0, META_PAD - N_META), (0, 0))
    pa_m = jnp.pad(pa_m, pad)
    pb_m = jnp.pad(pb_m, pad)
    vta_m = _extend_heads(jnp.pad(va_m, pad).T.reshape(a_heads, A_V_DIM, META_PAD), 1)
    vtb_m = _extend_heads(jnp.pad(vb_m, pad).T.reshape(b_kv, HEAD_DIM, META_PAD), 1)

    slopes = jnp.asarray([2.0 ** (-8.0 * (i + 1) / a_heads) for i in range(a_heads)], F32)
    ya = _attn_a(pa, vta, pa_m, vta_m, slopes, lambda_q1, lambda_k1, lambda_q2, lambda_k2,
                 subln_gain[0][:, None], batch, seq, a_heads,
                 tq=A_QUERY_TILE, tk=A_KEY_CHUNK)
    yb = _attn_b(pb, vtb, pb_m, vtb_m, sink_logits[0], batch, seq, b_heads, b_kv,
                 nblk=B_BLOCKS_PER_STEP, slab=A_KEY_CHUNK)

    out = _post(x2d, ya, yb, gates,
                w_branch_a[0].astype(BF16), w_branch_b[0].astype(BF16), w_out[0].astype(BF16),
                norm_ffn[0][None, :], w_ff_gate[0].astype(BF16), w_ff_up[0].astype(BF16),
                w_ff_down[0].astype(BF16), norm_final[None, :], tm=POST_ROW_TILE)
    return out.reshape(batch, seq, d)
```

```python
import functools
import math

import ml_dtypes
import numpy as np

import jax
import jax.numpy as jnp
from jax import lax
from jax.experimental import pallas as pl
from jax.experimental.pallas import tpu as pltpu

F32 = jnp.float32
BF16 = jnp.bfloat16

N_META = 16
HEAD_DIM = 64
WINDOW = 128
BLOCK = 128
A_V_DIM = 2 * HEAD_DIM
EPS = 1e-6
LAMBDA_INIT = 0.8 - 0.6 * math.exp(-0.3 * 0)
SCALE = HEAD_DIM ** -0.5
LOG2E = math.log2(math.e)

LANES = 128
BF16_SUBLANES = 16
META_PAD = 128
NEG = -1e30
VMEM_LIMIT_BYTES = 56 * 1024 * 1024

IN_PROJ_ROW_TILE = 512
A_KEY_CHUNK = 512
A_QUERY_TILE = 512
A_SCORES_AHEAD = 2
A_STREAMS = 1
A_KEY_PARTS = 1
POST_ROW_TILE = 512
FFN_COLUMN_GROUP = 1536

A_VT_ROWS = A_V_DIM + BF16_SUBLANES
B_VT_ROWS = HEAD_DIM + BF16_SUBLANES
B_BLOCKS_PER_STEP = 4
_BLOCK_BITS = BLOCK.bit_length() - 1

_DN_NT = (((1,), (1,)), ((), ()))


def _resident(block_shape, index_map):
    return pl.BlockSpec(block_shape, index_map, pipeline_mode=pl.Buffered(1))


def _in_proj_kernel(x_ref, g_ref, w_ref, wvt_ref, *out_refs, splits, gates_from, vt_splits,
                    ones_rows):
    x = x_ref[...]
    ms = jnp.mean(x * x, axis=-1, keepdims=True)
    hn = ((x * lax.rsqrt(ms + EPS)) * g_ref[...]).astype(BF16)
    for (lo, hi), o_ref in zip(splits, out_refs):
        o_ref[...] = jnp.dot(hn, w_ref[:, lo:hi], preferred_element_type=F32).astype(BF16)
    if gates_from is not None:
        vt_refs = out_refs[len(splits):-1]
        vt = lax.dot_general(wvt_ref[...], hn, _DN_NT, preferred_element_type=F32)
        row = lax.broadcasted_iota(jnp.int32, vt.shape, 0)
        is_ones = functools.reduce(jnp.logical_or, [row == r for r in ones_rows])
        vt = jnp.where(is_ones, 1.0, vt).astype(BF16)
        for (lo, hi, slab), vt_ref in zip(vt_splits, vt_refs):
            for j in range(vt.shape[1] // slab):
                vt_ref[j] = vt[lo:hi, j * slab:(j + 1) * slab]
        g = jnp.dot(hn, w_ref[:, gates_from:], preferred_element_type=F32)
        out_refs[-1][...] = jax.nn.sigmoid(g).astype(BF16)


def _in_proj(x2d, gain, w_bf16, wvt_bf16, splits, gates_from, vt_splits, ones_rows, tm):
    m, d = x2d.shape
    n_all = w_bf16.shape[1]
    nv = wvt_bf16.shape[0]
    widths = [hi - lo for lo, hi in splits]
    out_shape = [jax.ShapeDtypeStruct((m, w), BF16) for w in widths]
    out_specs = [pl.BlockSpec((tm, w), lambda i: (i, 0)) for w in widths]
    if gates_from is not None:
        for lo, hi, slab in vt_splits:
            out_shape.append(jax.ShapeDtypeStruct((m // slab, hi - lo, slab), BF16))
            out_specs.append(pl.BlockSpec((tm // slab, hi - lo, slab), lambda i: (i, 0, 0)))
        out_shape.append(jax.ShapeDtypeStruct((m, n_all - gates_from), BF16))
        out_specs.append(pl.BlockSpec((tm, n_all - gates_from), lambda i: (i, 0)))
    return pl.pallas_call(
        functools.partial(_in_proj_kernel, splits=splits, gates_from=gates_from,
                          vt_splits=vt_splits, ones_rows=ones_rows),
        grid=(m // tm,),
        in_specs=[
            pl.BlockSpec((tm, d), lambda i: (i, 0)),
            _resident((1, d), lambda i: (0, 0)),
            _resident((d, n_all), lambda i: (0, 0)),
            _resident((nv, d), lambda i: (0, 0)),
        ],
        out_specs=out_specs,
        out_shape=out_shape,
        compiler_params=pltpu.CompilerParams(
            dimension_semantics=("arbitrary",), vmem_limit_bytes=VMEM_LIMIT_BYTES),
        name="in_proj_meta" if gates_from is None else "in_proj_gates",
    )(x2d, gain, w_bf16, wvt_bf16)


_FEAT_SPLIT_BITS = 6
_N_PIECES = 3


def _bf16_pieces(x, n):
    pieces, rest = [], np.float32(x)
    for _ in range(n):
        piece = np.float32(np.asarray(rest, dtype=ml_dtypes.bfloat16))
        pieces.append(piece)
        rest = np.float32(rest - piece)
    return pieces


def _feature_table():
    e = _bf16_pieces(LOG2E, _N_PIECES)
    t = np.zeros((8, LANES), np.float32)
    n = _N_PIECES
    t[0, 0:n] = 1.0
    t[1, n:2 * n] = 1.0
    t[2, 2 * n:3 * n] = e
    t[2, 3 * n:4 * n] = e
    t[3, 2 * n:3 * n] = 1.0
    t[4, 3 * n:4 * n] = 1.0
    t[5, 0:n] = e
    t[5, n:2 * n] = e
    return jnp.asarray(t)


def _pos_terms(pos, slope):
    hi = (pos >> _FEAT_SPLIT_BITS).astype(F32) * (slope * (1 << _FEAT_SPLIT_BITS))
    lo = (pos & ((1 << _FEAT_SPLIT_BITS) - 1)).astype(F32) * slope
    return hi, lo


def _attn_a_kernel(slopes_ref, lq1_ref, lk1_ref, lq2_ref, lk2_ref, gain_ref, feat_ref,
                   q_ref, k_ref, vt_ref, km_ref, vtm_ref, o_ref,
                   kaug_ref, qfeat_ref, negabs_ref, *, tq, tk, seq, n_streams):
    b = pl.program_id(0)
    h = pl.program_id(1)
    qi = pl.program_id(2)
    slope = slopes_ref[h]
    ts = tq // n_streams
    nqc = 2 * ts
    n_chunks = seq // tk
    n_off = max(1, tk // ts)
    n_diag = max(1, ts // tk)
    tkp = tk // A_KEY_PARTS

    @pl.when((b == 0) & (h == 0) & (qi == 0))
    def _build_distance_tiles():
        for v in range(max(n_off, n_diag)):
            start = v * ts if tk >= ts else -v * tk
            d = ((lax.broadcasted_iota(jnp.int32, (tk, nqc), 1) & (ts - 1))
                 - lax.broadcasted_iota(jnp.int32, (tk, nqc), 0) + start)
            negabs_ref[v] = -jnp.abs(d).astype(F32)

    @pl.when(qi == 0)
    def _build_features():
        def fill(c, carry):
            rows = pl.ds(pl.multiple_of(c * tk, tk), tk)
            pos = c * tk + lax.broadcasted_iota(jnp.int32, (tk, LANES), 0)
            hi, lo = _pos_terms(pos, slope)
            kaug_ref[rows, :LANES] = k_ref[rows, :]
            kf = hi * feat_ref[0:1, :] + lo * feat_ref[1:2, :] + feat_ref[2:3, :]
            kaug_ref[rows, LANES:] = kf.astype(BF16)
            qf = feat_ref[5:6, :] - hi * feat_ref[3:4, :] - lo * feat_ref[4:5, :]
            qfeat_ref[rows, :] = qf.astype(BF16)
            return carry
        lax.fori_loop(0, n_chunks, fill, 0)

    def scores(k_rows, q_aug_t):
        return jnp.dot(k_rows, q_aug_t, preferred_element_type=F32)

    lam = (jnp.exp(jnp.sum(lq1_ref[...] * lk1_ref[...], axis=-1, keepdims=True))
           - jnp.exp(jnp.sum(lq2_ref[...] * lk2_ref[...], axis=-1, keepdims=True))
           + LAMBDA_INIT)
    km_aug = jnp.concatenate([km_ref[...], jnp.zeros((META_PAD, LANES), BF16)], axis=1)
    meta_valid = lax.broadcasted_iota(jnp.int32, (META_PAD, nqc), 0) < N_META
    lane = lax.broadcasted_iota(jnp.int32, (ts, LANES), 1)

    class Stream:
        def __init__(self, u):
            self.u = u
            self.qs = qi * n_streams + u
            self.c_diag = (self.qs * ts) // tk
            q = q_ref[u * ts:(u + 1) * ts, :]
            zero = jnp.zeros_like(q)
            qq = jnp.concatenate([jnp.where(lane < HEAD_DIM, q, zero),
                                  jnp.where(lane >= HEAD_DIM, q, zero)], axis=0)
            qf = qfeat_ref[pl.ds(pl.multiple_of(self.qs * ts, ts), ts), :]
            qf = jnp.concatenate([qf, qf], axis=0)
            qq_t, qf_t = qq.T, qf.T
            self.q_left = jnp.concatenate([qq_t, qf_t], axis=0)
            self.q_right = jnp.concatenate([qq_t, -qf_t], axis=0)
            self.q_plain = jnp.concatenate([qq_t, jnp.zeros_like(qf_t)], axis=0)
            s = jnp.where(meta_valid, scores(km_aug, self.q_plain), NEG)
            self.m = jnp.max(s, axis=0, keepdims=True)
            self.acc = jnp.dot(vtm_ref[...], jnp.exp2(s - self.m).astype(BF16),
                               preferred_element_type=F32)
            self.pending = []

        def chunk_of(self, t):
            if t < n_diag:
                return self.c_diag + t
            u = t - n_diag
            return u + (u >= self.c_diag).astype(jnp.int32) * n_diag

        def scores_part(self, t, part):
            c = self.chunk_of(t)
            rows = pl.ds(pl.multiple_of(c * tk + part * tkp, tkp), tkp)
            if t < n_diag:
                variant = self.qs - self.c_diag * n_off if tk >= ts else t
                bias = negabs_ref[variant, part * tkp:(part + 1) * tkp, :]
                return scores(kaug_ref[rows, :], self.q_plain) + bias * (slope * LOG2E)
            return scores(kaug_ref[rows, :], jnp.where(c < self.c_diag, self.q_left, self.q_right))

        def issue_scores(self, t):
            self.pending.append(jnp.concatenate(
                [self.scores_part(t, part) for part in range(A_KEY_PARTS)], axis=0))

        def softmax(self):
            s = self.pending.pop(0)
            m_new = jnp.maximum(self.m, jnp.max(s, axis=0, keepdims=True))
            self.p = jnp.exp2(s - m_new).astype(BF16)
            self.alpha = jnp.exp2(self.m - m_new)
            self.m = m_new

        def pv_part(self, t, part):
            keys = slice(part * tkp, (part + 1) * tkp)
            return jnp.dot(vt_ref[self.chunk_of(t), :, keys], self.p[keys],
                           preferred_element_type=F32)

        def step(self, t):
            ahead = t + A_SCORES_AHEAD
            new_scores, pv = [], None
            for part in range(A_KEY_PARTS):
                if ahead < n_chunks:
                    new_scores.append(self.scores_part(ahead, part))
                part_pv = self.pv_part(t, part)
                pv = part_pv if pv is None else pv + part_pv
            self.acc = self.alpha * self.acc + pv
            if new_scores:
                self.pending.append(jnp.concatenate(new_scores, axis=0))

        def finish(self):
            o = self.acc[:A_V_DIM] / self.acc[A_V_DIM:A_V_DIM + 1]
            y = o[:, :ts] - lam * o[:, ts:]
            ms = jnp.mean(y * y, axis=0, keepdims=True)
            y = ((y * lax.rsqrt(ms + EPS)) * gain_ref[...]) * (1.0 - LAMBDA_INIT)
            o_ref[self.u * ts:(self.u + 1) * ts, :] = y.T.astype(BF16)

    for u in range(n_streams):
        st = Stream(u)
        for t in range(min(A_SCORES_AHEAD, n_chunks)):
            st.issue_scores(t)
        st.softmax()
        for t in range(n_chunks):
            st.step(t)
            if st.pending:
                st.softmax()
        st.finish()


def _attn_a(pa, vt, pa_meta, vt_meta, slopes, lq1, lk1, lq2, lk2, gain_col,
            batch, seq, heads, tq, tk):
    nq = seq // tq
    width = heads * A_V_DIM
    ts = tq // A_STREAMS
    assert tq % A_STREAMS == 0 and max(tk, ts) % min(tk, ts) == 0
    assert seq % tk == 0 and seq % tq == 0
    vec = lambda n: _resident((1, n), lambda b, h, i: (0, 0))
    return pl.pallas_call(
        functools.partial(_attn_a_kernel, tq=tq, tk=tk, seq=seq, n_streams=A_STREAMS),
        grid=(batch, heads, nq),
        in_specs=[
            pl.BlockSpec(memory_space=pltpu.SMEM),
            vec(HEAD_DIM), vec(HEAD_DIM), vec(HEAD_DIM), vec(HEAD_DIM),
            _resident((A_V_DIM, 1), lambda b, h, i: (0, 0)),
            _resident((8, LANES), lambda b, h, i: (0, 0)),
            pl.BlockSpec((tq, LANES), lambda b, h, i: (b * nq + i, h)),
            pl.BlockSpec((seq, LANES), lambda b, h, i: (b, heads + h)),
            pl.BlockSpec((seq // tk, A_VT_ROWS, tk), lambda b, h, i: (b, h, 0)),
            pl.BlockSpec((META_PAD, LANES), lambda b, h, i: (0, heads + h)),
            pl.BlockSpec((A_VT_ROWS, META_PAD), lambda b, h, i: (h, 0)),
        ],
        out_specs=pl.BlockSpec((tq, A_V_DIM), lambda b, h, i: (b * nq + i, h)),
        out_shape=jax.ShapeDtypeStruct((batch * seq, width), BF16),
        scratch_shapes=[pltpu.VMEM((seq, 2 * LANES), BF16),
                        pltpu.VMEM((seq, LANES), BF16),
                        pltpu.VMEM((max(tk // ts, ts // tk), tk, 2 * ts), F32)],
        compiler_params=pltpu.CompilerParams(
            dimension_semantics=("arbitrary", "arbitrary", "arbitrary"),
            vmem_limit_bytes=VMEM_LIMIT_BYTES),
        name="attn_a",
    )(slopes, lq1, lk1, lq2, lk2, gain_col, _feature_table(), pa, pa, vt, pa_meta, vt_meta)


def _attn_b_kernel(sink_ref, q_ref, kprev_ref, kcur_ref, knext_ref, vprev_ref, vcur_ref,
                   vnext_ref, km_ref, vtm_ref, o_ref, bias_ref, *, seq, heads, group, nblk):
    b = pl.program_id(0)
    step = pl.program_id(1)
    nb = seq // BLOCK
    kv_heads = heads // group
    nwin = 3 * BLOCK
    nkeys = nwin + N_META
    ncols = heads * BLOCK
    gcols = group * BLOCK
    k_pad = -(-nkeys // (2 * LANES)) * (2 * LANES)

    @pl.when((b == 0) & (step == 0))
    def _build_bias():
        kr = lax.broadcasted_iota(jnp.int32, (nkeys, ncols), 0)
        col = lax.broadcasted_iota(jnp.int32, (nkeys, ncols), 1)
        dt = jnp.abs((col & (BLOCK - 1)) - (kr - BLOCK))
        head = (col >> _BLOCK_BITS).astype(F32)
        slope = jnp.exp2(-(head + 1.0) * (8.0 / heads))
        base = jnp.where((kr < nwin) & (dt <= WINDOW), dt.astype(F32) * (-LOG2E) * slope,
                         jnp.where(kr >= nwin, 0.0, NEG))
        bias_ref[0] = base
        bias_ref[1] = jnp.where(kr < BLOCK, NEG, base)
        bias_ref[2] = jnp.where((kr >= 2 * BLOCK) & (kr < nwin), NEG, base)

    col_head = lax.broadcasted_iota(jnp.int32, (1, ncols), 1) >> _BLOCK_BITS
    sink_row = jnp.zeros((1, ncols), F32)
    for hd in range(heads):
        sink_row = jnp.where(col_head == hd, sink_ref[hd] * LOG2E, sink_row)

    lane = lax.broadcasted_iota(jnp.int32, (BLOCK, LANES), 1)
    lo = lane < HEAD_DIM
    zero = jnp.zeros((BLOCK, LANES), BF16)
    k_blocks = ([kprev_ref[...]]
                + [kcur_ref[j * BLOCK:(j + 1) * BLOCK, :] for j in range(nblk)] + [knext_ref[...]])
    vt_blocks = ([vprev_ref[0]]
                 + [vcur_ref[0, :, j * BLOCK:(j + 1) * BLOCK] for j in range(nblk)] + [vnext_ref[0]])
    k_meta = km_ref[:N_META, :]
    vt_meta = vtm_ref[...]

    all_scores = []
    for blk in range(nblk):
        i = step * nblk + blk
        k_win = jnp.concatenate(k_blocks[blk:blk + 3] + [k_meta], axis=0)

        q = q_ref[blk * BLOCK:(blk + 1) * BLOCK, :]
        q_heads = []
        for hd in range(heads):
            part = q[:, (hd // 2) * LANES:(hd // 2 + 1) * LANES]
            in_lo, want_lo = hd % 2 == 0, (hd // group) % 2 == 0
            if in_lo != want_lo:
                part = jnp.concatenate([part[:, HEAD_DIM:], part[:, :HEAD_DIM]], axis=1)
            q_heads.append(jnp.where(lo if want_lo else ~lo, part, zero))
        q_all = jnp.concatenate(q_heads, axis=0)

        variant = jnp.where(i == 0, 1, jnp.where(i == nb - 1, 2, 0))
        all_scores.append(
            lax.dot_general(k_win, q_all, _DN_NT, preferred_element_type=F32) + bias_ref[variant])

    for blk, s in enumerate(all_scores):
        vt_win = jnp.concatenate(vt_blocks[blk:blk + 3] + [vt_meta], axis=1)
        m = jnp.maximum(jnp.max(s, axis=0, keepdims=True), sink_row)
        p = jnp.exp2(s - m).astype(BF16)
        p = jnp.concatenate([p, jnp.zeros((k_pad - nkeys, ncols), BF16)], axis=0)
        sink_term = jnp.exp2(sink_row - m)

        outs = []
        for g in range(kv_heads):
            cols = slice(g * gcols, (g + 1) * gcols)
            og = jnp.dot(vt_win[g * B_VT_ROWS:(g + 1) * B_VT_ROWS], p[:, cols],
                         preferred_element_type=F32)
            outs.append(og[:HEAD_DIM] / (og[HEAD_DIM:HEAD_DIM + 1] + sink_term[:, cols]))
        pairs = []
        for j in range(heads // 2):
            g, first = (2 * j) // group, (2 * j) % group
            pair = outs[g][:, first * BLOCK:(first + 2) * BLOCK]
            pairs.append(jnp.concatenate([pair[:, :BLOCK], pair[:, BLOCK:]], axis=0).T)
        o_ref[blk * BLOCK:(blk + 1) * BLOCK, :] = jnp.concatenate(pairs, axis=1).astype(BF16)


def _attn_b(pb, vtb, pb_meta, vtb_meta, sink, batch, seq, heads, kv_heads, nblk, slab):
    nb = seq // BLOCK
    n_steps = nb // nblk
    qw = heads * HEAD_DIM
    kcol = qw // LANES
    rows = kv_heads * B_VT_ROWS
    per_slab = slab // BLOCK
    assert kv_heads * HEAD_DIM == LANES and nb % nblk == 0 and per_slab % nblk == 0
    prev_blk = lambda i: jnp.maximum(nblk * i - 1, 0)
    next_blk = lambda i: jnp.minimum(nblk * i + nblk, nb - 1)
    vt_spec = lambda blk_of: pl.BlockSpec(
        (1, rows, BLOCK),
        lambda b, i: (b * (nb // per_slab) + blk_of(i) // per_slab, 0, blk_of(i) % per_slab))
    return pl.pallas_call(
        functools.partial(_attn_b_kernel, seq=seq, heads=heads, group=heads // kv_heads, nblk=nblk),
        grid=(batch, n_steps),
        in_specs=[
            pl.BlockSpec(memory_space=pltpu.SMEM),
            pl.BlockSpec((nblk * BLOCK, qw), lambda b, i: (b * n_steps + i, 0)),
            pl.BlockSpec((BLOCK, LANES), lambda b, i: (b * nb + prev_blk(i), kcol)),
            pl.BlockSpec((nblk * BLOCK, LANES), lambda b, i: (b * n_steps + i, kcol)),
            pl.BlockSpec((BLOCK, LANES), lambda b, i: (b * nb + next_blk(i), kcol)),
            vt_spec(prev_blk),
            pl.BlockSpec((1, rows, nblk * BLOCK),
                         lambda b, i: (b * (nb // per_slab) + (nblk * i) // per_slab, 0,
                                       i % (per_slab // nblk))),
            vt_spec(next_blk),
            _resident((META_PAD, LANES), lambda b, i: (0, kcol)),
            _resident((rows, META_PAD), lambda b, i: (0, 0)),
        ],
        out_specs=pl.BlockSpec((nblk * BLOCK, qw), lambda b, i: (b * n_steps + i, 0)),
        out_shape=jax.ShapeDtypeStruct((batch * seq, qw), BF16),
        scratch_shapes=[pltpu.VMEM((3, 3 * BLOCK + N_META, heads * BLOCK), F32)],
        compiler_params=pltpu.CompilerParams(
            dimension_semantics=("arbitrary", "arbitrary"),
            vmem_limit_bytes=VMEM_LIMIT_BYTES),
        name="attn_b",
    )(sink, pb, pb, pb, pb, vtb, vtb, vtb, pb_meta, vtb_meta)


def _rms(x, g):
    ms = jnp.mean(x * x, axis=-1, keepdims=True)
    return (x * lax.rsqrt(ms + EPS)) * g


def _post_kernel(x_ref, ya_ref, yb_ref, gate_ref, wba_ref, wbb_ref, wo_ref, nf_ref,
                 wg_ref, wu_ref, wd_ref, nfin_ref, o_ref, *, d_model):
    dot = functools.partial(jnp.dot, preferred_element_type=F32)
    ga = gate_ref[:, :d_model].astype(F32)
    gb = gate_ref[:, d_model:].astype(F32)
    merged = ga * dot(ya_ref[...], wba_ref[...]) + gb * dot(yb_ref[...], wbb_ref[...])
    h1 = x_ref[...] + dot(merged.astype(BF16), wo_ref[...])
    hn = _rms(h1, nf_ref[...]).astype(BF16)
    d_ff = wg_ref.shape[1]
    bounds = [min(k * FFN_COLUMN_GROUP, d_ff) for k in range(-(-d_ff // FFN_COLUMN_GROUP) + 1)]
    h2 = h1
    for lo, hi in zip(bounds[:-1], bounds[1:]):
        act = jax.nn.silu(dot(hn, wg_ref[:, lo:hi])) * dot(hn, wu_ref[:, lo:hi])
        h2 = h2 + dot(act.astype(BF16), wd_ref[lo:hi, :])
    o_ref[...] = _rms(h2, nfin_ref[...])


def _post(x2d, ya, yb, gates, wba, wbb, wo, nf, wg, wu, wd, nfin, tm):
    m, d = x2d.shape
    row = lambda w: pl.BlockSpec((tm, w), lambda i: (i, 0))
    full = lambda a: _resident(a.shape, lambda i: (0, 0))
    return pl.pallas_call(
        functools.partial(_post_kernel, d_model=d),
        grid=(m // tm,),
        in_specs=[row(d), row(ya.shape[1]), row(yb.shape[1]), row(2 * d),
                  full(wba), full(wbb), full(wo), full(nf),
                  full(wg), full(wu), full(wd), full(nfin)],
        out_specs=row(d),
        out_shape=jax.ShapeDtypeStruct((m, d), F32),
        compiler_params=pltpu.CompilerParams(
            dimension_semantics=("arbitrary",), vmem_limit_bytes=VMEM_LIMIT_BYTES),
        name="post_ffn",
    )(x2d, ya, yb, gates, wba, wbb, wo, nf, wg, wu, wd, nfin)


def _extend_heads(vt_heads, first_row):
    heads, dim, n = vt_heads.shape
    extra = jnp.zeros((heads, BF16_SUBLANES, n), vt_heads.dtype).at[:, 0, :].set(first_row)
    return jnp.concatenate([vt_heads, extra], axis=1).reshape(heads * (dim + BF16_SUBLANES), n)


def kernel(x, meta_tokens, norm_mix, w_in, lambda_q1, lambda_k1, lambda_q2, lambda_k2,
           subln_gain, sink_logits, w_branch_a, w_branch_b, w_out, norm_ffn,
           w_ff_gate, w_ff_up, w_ff_down, norm_final):
    batch, seq, d = x.shape
    assert norm_mix.shape[0] == 1, "single-layer block"
    a_heads = w_branch_a.shape[1] // A_V_DIM
    b_heads = sink_logits.shape[1]
    b_kv = max(1, b_heads // 4)
    qa_w = a_heads * 2 * HEAD_DIM
    qk_w = 2 * qa_w
    va_w = a_heads * A_V_DIM
    qb_w = b_heads * HEAD_DIM
    kvb_w = b_kv * HEAD_DIM
    pb_lo = qk_w + va_w
    vb_lo = pb_lo + qb_w + kvb_w
    gates_from = vb_lo + kvb_w

    x2d = x.reshape(batch * seq, d)
    col_scale = (jnp.ones((w_in.shape[2],), F32)
                 .at[:qa_w].set(SCALE * LOG2E).at[pb_lo:pb_lo + qb_w].set(SCALE * LOG2E))
    w_in_b = (w_in[0] * col_scale).astype(BF16)
    wvt_a = _extend_heads(w_in_b[:, qk_w:pb_lo].T.reshape(a_heads, A_V_DIM, d), 0)
    wvt_b = _extend_heads(w_in_b[:, vb_lo:gates_from].T.reshape(b_kv, HEAD_DIM, d), 0)
    wvt = jnp.concatenate([wvt_a, wvt_b], axis=0)
    na, nbt = wvt_a.shape[0], wvt_b.shape[0]
    ones_rows = (tuple(range(A_V_DIM, na, A_VT_ROWS))
                 + tuple(range(na + HEAD_DIM, na + nbt, B_VT_ROWS)))
    gain = norm_mix[0][None, :]

    pa, pb, vta, vtb, gates = _in_proj(
        x2d, gain, w_in_b, wvt, ((0, qk_w), (pb_lo, vb_lo)), gates_from,
        ((0, na, A_KEY_CHUNK), (na, na + nbt, IN_PROJ_ROW_TILE)), ones_rows, tm=IN_PROJ_ROW_TILE)
    pa_m, va_m, pb_m, vb_m = _in_proj(
        meta_tokens.astype(F32), gain, w_in_b, wvt,
        ((0, qk_w), (qk_w, pb_lo), (pb_lo, vb_lo), (vb_lo, gates_from)), None, (), (), tm=N_META)
    pad = ((0, META_PAD - N_META), (0, 0))
    pa_m = jnp.pad(pa_m, pad)
    pb_m = jnp.pad(pb_m, pad)
    vta_m = _extend_heads(jnp.pad(va_m, pad).T.reshape(a_heads, A_V_DIM, META_PAD), 1)
    vtb_m = _extend_heads(jnp.pad(vb_m, pad).T.reshape(b_kv, HEAD_DIM, META_PAD), 1)

    slopes = jnp.asarray([2.0 ** (-8.0 * (i + 1) / a_heads) for i in range(a_heads)], F32)
    ya = _attn_a(pa, vta, pa_m, vta_m, slopes, lambda_q1, lambda_k1, lambda_q2, lambda_k2,
                 subln_gain[0][:, None], batch, seq, a_heads,
                 tq=A_QUERY_TILE, tk=A_KEY_CHUNK)
    yb = _attn_b(pb, vtb, pb_m, vtb_m, sink_logits[0], batch, seq, b_heads, b_kv,
                 nblk=B_BLOCKS_PER_STEP, slab=IN_PROJ_ROW_TILE)

    out = _post(x2d, ya, yb, gates,
                w_branch_a[0].astype(BF16), w_branch_b[0].astype(BF16), w_out[0].astype(BF16),
                norm_ffn[0][None, :], w_ff_gate[0].astype(BF16), w_ff_up[0].astype(BF16),
                w_ff_down[0].astype(BF16), norm_final[None, :], tm=POST_ROW_TILE)
    return out.reshape(batch, seq, d)
```

```python
import functools
import math

import ml_dtypes
import numpy as np

import jax
import jax.numpy as jnp
from jax import lax
from jax.experimental import pallas as pl
from jax.experimental.pallas import tpu as pltpu

F32 = jnp.float32
BF16 = jnp.bfloat16

N_META = 16
HEAD_DIM = 64
WINDOW = 128
BLOCK = 128
A_V_DIM = 2 * HEAD_DIM
EPS = 1e-6
LAMBDA_INIT = 0.8 - 0.6 * math.exp(-0.3 * 0)
SCALE = HEAD_DIM ** -0.5
LOG2E = math.log2(math.e)

LANES = 128
BF16_SUBLANES = 16
META_PAD = 128
NEG = -1e30
VMEM_LIMIT_BYTES = 56 * 1024 * 1024

IN_PROJ_ROW_TILE = 512
A_KEY_CHUNK = 512
A_QUERY_TILE = 512
A_SCORES_AHEAD = 2
POST_ROW_TILE = 512
POST_ROW_GROUPS = 2
FFN_COLUMN_GROUP = 1536

A_VT_ROWS = A_V_DIM + BF16_SUBLANES
B_VT_ROWS = HEAD_DIM + BF16_SUBLANES
B_BLOCKS_PER_STEP = 4
_BLOCK_BITS = BLOCK.bit_length() - 1

_DN_NT = (((1,), (1,)), ((), ()))


def _resident(block_shape, index_map):
    return pl.BlockSpec(block_shape, index_map, pipeline_mode=pl.Buffered(1))


def _in_proj_kernel(x_ref, g_ref, w_ref, wvt_ref, *out_refs, splits, gates_from, vt_splits,
                    ones_rows):
    x = x_ref[...]
    ms = jnp.mean(x * x, axis=-1, keepdims=True)
    hn = ((x * lax.rsqrt(ms + EPS)) * g_ref[...]).astype(BF16)
    if gates_from is not None:
        g = jnp.dot(hn, w_ref[:, gates_from:], preferred_element_type=F32)
        out_refs[-1][...] = jax.nn.sigmoid(g).astype(BF16)
    for (lo, hi), o_ref in zip(splits, out_refs):
        o_ref[...] = jnp.dot(hn, w_ref[:, lo:hi], preferred_element_type=F32).astype(BF16)
    if gates_from is not None:
        vt_refs = out_refs[len(splits):-1]
        vt = lax.dot_general(wvt_ref[...], hn, _DN_NT, preferred_element_type=F32)
        row = lax.broadcasted_iota(jnp.int32, vt.shape, 0)
        is_ones = functools.reduce(jnp.logical_or, [row == r for r in ones_rows])
        vt = jnp.where(is_ones, 1.0, vt).astype(BF16)
        for (lo, hi, slab), vt_ref in zip(vt_splits, vt_refs):
            for j in range(vt.shape[1] // slab):
                vt_ref[j] = vt[lo:hi, j * slab:(j + 1) * slab]


def _in_proj(x2d, gain, w_bf16, wvt_bf16, splits, gates_from, vt_splits, ones_rows, tm):
    m, d = x2d.shape
    n_all = w_bf16.shape[1]
    nv = wvt_bf16.shape[0]
    widths = [hi - lo for lo, hi in splits]
    out_shape = [jax.ShapeDtypeStruct((m, w), BF16) for w in widths]
    out_specs = [pl.BlockSpec((tm, w), lambda i: (i, 0)) for w in widths]
    if gates_from is not None:
        for lo, hi, slab in vt_splits:
            out_shape.append(jax.ShapeDtypeStruct((m // slab, hi - lo, slab), BF16))
            out_specs.append(pl.BlockSpec((tm // slab, hi - lo, slab), lambda i: (i, 0, 0)))
        out_shape.append(jax.ShapeDtypeStruct((m, n_all - gates_from), BF16))
        out_specs.append(pl.BlockSpec((tm, n_all - gates_from), lambda i: (i, 0)))
    return pl.pallas_call(
        functools.partial(_in_proj_kernel, splits=splits, gates_from=gates_from,
                          vt_splits=vt_splits, ones_rows=ones_rows),
        grid=(m // tm,),
        in_specs=[
            pl.BlockSpec((tm, d), lambda i: (i, 0)),
            _resident((1, d), lambda i: (0, 0)),
            _resident((d, n_all), lambda i: (0, 0)),
            _resident((nv, d), lambda i: (0, 0)),
        ],
        out_specs=out_specs,
        out_shape=out_shape,
        compiler_params=pltpu.CompilerParams(
            dimension_semantics=("arbitrary",), vmem_limit_bytes=VMEM_LIMIT_BYTES),
        name="in_proj_meta" if gates_from is None else "in_proj_gates",
    )(x2d, gain, w_bf16, wvt_bf16)


_FEAT_SPLIT_BITS = 6
_N_PIECES = 3


def _bf16_pieces(x, n):
    pieces, rest = [], np.float32(x)
    for _ in range(n):
        piece = np.float32(np.asarray(rest, dtype=ml_dtypes.bfloat16))
        pieces.append(piece)
        rest = np.float32(rest - piece)
    return pieces


def _feature_table():
    e = _bf16_pieces(LOG2E, _N_PIECES)
    t = np.zeros((8, LANES), np.float32)
    n = _N_PIECES
    t[0, 0:n] = 1.0
    t[1, n:2 * n] = 1.0
    t[2, 2 * n:3 * n] = e
    t[2, 3 * n:4 * n] = e
    t[3, 2 * n:3 * n] = 1.0
    t[4, 3 * n:4 * n] = 1.0
    t[5, 0:n] = e
    t[5, n:2 * n] = e
    return jnp.asarray(t)


def _pos_terms(pos, slope):
    hi = (pos >> _FEAT_SPLIT_BITS).astype(F32) * (slope * (1 << _FEAT_SPLIT_BITS))
    lo = (pos & ((1 << _FEAT_SPLIT_BITS) - 1)).astype(F32) * slope
    return hi, lo


def _attn_a_kernel(slopes_ref, lq1_ref, lk1_ref, lq2_ref, lk2_ref, gain_ref, feat_ref,
                   q_ref, k_ref, vt_ref, km_ref, vtm_ref, o_ref,
                   kaug_ref, qfeat_ref, negabs_ref, *, tq, tk, seq):
    b = pl.program_id(0)
    h = pl.program_id(1)
    qi = pl.program_id(2)
    slope = slopes_ref[h]
    nqc = 2 * tq
    n_chunks = seq // tk
    n_off = max(1, tk // tq)
    n_diag = max(1, tq // tk)

    @pl.when((b == 0) & (h == 0) & (qi == 0))
    def _build_distance_tiles():
        for v in range(max(n_off, n_diag)):
            start = v * tq if tk >= tq else -v * tk
            d = ((lax.broadcasted_iota(jnp.int32, (tk, nqc), 1) & (tq - 1))
                 - lax.broadcasted_iota(jnp.int32, (tk, nqc), 0) + start)
            negabs_ref[v] = -jnp.abs(d).astype(F32)

    @pl.when(qi == 0)
    def _build_features():
        def fill(c, carry):
            rows = pl.ds(pl.multiple_of(c * tk, tk), tk)
            pos = c * tk + lax.broadcasted_iota(jnp.int32, (tk, LANES), 0)
            hi, lo = _pos_terms(pos, slope)
            kaug_ref[rows, :LANES] = k_ref[rows, :]
            kf = hi * feat_ref[0:1, :] + lo * feat_ref[1:2, :] + feat_ref[2:3, :]
            kaug_ref[rows, LANES:] = kf.astype(BF16)
            qf = feat_ref[5:6, :] - hi * feat_ref[3:4, :] - lo * feat_ref[4:5, :]
            qfeat_ref[rows, :] = qf.astype(BF16)
            return carry
        lax.fori_loop(0, n_chunks, fill, 0)

    q = q_ref[...]
    lane = lax.broadcasted_iota(jnp.int32, (tq, LANES), 1)
    zero = jnp.zeros_like(q)
    qq = jnp.concatenate([jnp.where(lane < HEAD_DIM, q, zero),
                          jnp.where(lane >= HEAD_DIM, q, zero)], axis=0)
    qf = qfeat_ref[pl.ds(pl.multiple_of(qi * tq, tq), tq), :]
    qf = jnp.concatenate([qf, qf], axis=0)
    qq_t, qf_t = qq.T, qf.T
    q_left = jnp.concatenate([qq_t, qf_t], axis=0)
    q_right = jnp.concatenate([qq_t, -qf_t], axis=0)
    q_plain = jnp.concatenate([qq_t, jnp.zeros_like(qf_t)], axis=0)

    def scores(k_rows, q_aug_t):
        return jnp.dot(k_rows, q_aug_t, preferred_element_type=F32)

    c_diag = (qi * tq) // tk

    def chunk_of(t):
        if t < n_diag:
            return c_diag + t
        u = t - n_diag
        return u + (u >= c_diag).astype(jnp.int32) * n_diag

    def chunk_scores(t):
        c = chunk_of(t)
        k_rows = kaug_ref[pl.ds(pl.multiple_of(c * tk, tk), tk), :]
        if t < n_diag:
            variant = qi - c_diag * n_off if tk >= tq else t
            return scores(k_rows, q_plain) + negabs_ref[variant] * (slope * LOG2E)
        return scores(k_rows, jnp.where(c < c_diag, q_left, q_right))

    def softmax(s, m):
        m_new = jnp.maximum(m, jnp.max(s, axis=0, keepdims=True))
        return m_new, jnp.exp2(s - m_new).astype(BF16), jnp.exp2(m - m_new)

    km_aug = jnp.concatenate([km_ref[...], jnp.zeros((META_PAD, LANES), BF16)], axis=1)
    meta_valid = lax.broadcasted_iota(jnp.int32, (META_PAD, nqc), 0) < N_META
    s = jnp.where(meta_valid, scores(km_aug, q_plain), NEG)
    m = jnp.max(s, axis=0, keepdims=True)
    acc = jnp.dot(vtm_ref[...], jnp.exp2(s - m).astype(BF16), preferred_element_type=F32)

    pending = [chunk_scores(t) for t in range(min(A_SCORES_AHEAD, n_chunks))]
    m, p, alpha = softmax(pending.pop(0), m)
    for t in range(n_chunks):
        if t + A_SCORES_AHEAD < n_chunks:
            pending.append(chunk_scores(t + A_SCORES_AHEAD))
        acc = alpha * acc + jnp.dot(vt_ref[chunk_of(t)], p, preferred_element_type=F32)
        if pending:
            m, p, alpha = softmax(pending.pop(0), m)

    lam = (jnp.exp(jnp.sum(lq1_ref[...] * lk1_ref[...], axis=-1, keepdims=True))
           - jnp.exp(jnp.sum(lq2_ref[...] * lk2_ref[...], axis=-1, keepdims=True))
           + LAMBDA_INIT)
    o = acc[:A_V_DIM] / acc[A_V_DIM:A_V_DIM + 1]
    y = o[:, :tq] - lam * o[:, tq:]
    ms = jnp.mean(y * y, axis=0, keepdims=True)
    y = ((y * lax.rsqrt(ms + EPS)) * gain_ref[...]) * (1.0 - LAMBDA_INIT)
    o_ref[...] = y.T.astype(BF16)


def _attn_a(pa, vt, pa_meta, vt_meta, slopes, lq1, lk1, lq2, lk2, gain_col,
            batch, seq, heads, tq, tk):
    nq = seq // tq
    width = heads * A_V_DIM
    assert max(tk, tq) % min(tk, tq) == 0 and seq % tk == 0 and seq % tq == 0
    vec = lambda n: _resident((1, n), lambda b, h, i: (0, 0))
    return pl.pallas_call(
        functools.partial(_attn_a_kernel, tq=tq, tk=tk, seq=seq),
        grid=(batch, heads, nq),
        in_specs=[
            pl.BlockSpec(memory_space=pltpu.SMEM),
            vec(HEAD_DIM), vec(HEAD_DIM), vec(HEAD_DIM), vec(HEAD_DIM),
            _resident((A_V_DIM, 1), lambda b, h, i: (0, 0)),
            _resident((8, LANES), lambda b, h, i: (0, 0)),
            pl.BlockSpec((tq, LANES), lambda b, h, i: (b * nq + i, h)),
            pl.BlockSpec((seq, LANES), lambda b, h, i: (b, heads + h)),
            pl.BlockSpec((seq // tk, A_VT_ROWS, tk), lambda b, h, i: (b, h, 0)),
            pl.BlockSpec((META_PAD, LANES), lambda b, h, i: (0, heads + h)),
            pl.BlockSpec((A_VT_ROWS, META_PAD), lambda b, h, i: (h, 0)),
        ],
        out_specs=pl.BlockSpec((tq, A_V_DIM), lambda b, h, i: (b * nq + i, h)),
        out_shape=jax.ShapeDtypeStruct((batch * seq, width), BF16),
        scratch_shapes=[pltpu.VMEM((seq, 2 * LANES), BF16),
                        pltpu.VMEM((seq, LANES), BF16),
                        pltpu.VMEM((max(tk // tq, tq // tk), tk, 2 * tq), F32)],
        compiler_params=pltpu.CompilerParams(
            dimension_semantics=("arbitrary", "arbitrary", "arbitrary"),
            vmem_limit_bytes=VMEM_LIMIT_BYTES),
        name="attn_a",
    )(slopes, lq1, lk1, lq2, lk2, gain_col, _feature_table(), pa, pa, vt, pa_meta, vt_meta)


def _attn_b_kernel(sink_ref, q_ref, kprev_ref, kcur_ref, knext_ref, vprev_ref, vcur_ref,
                   vnext_ref, km_ref, vtm_ref, o_ref, bias_ref, *, seq, heads, group, nblk):
    b = pl.program_id(0)
    step = pl.program_id(1)
    nb = seq // BLOCK
    kv_heads = heads // group
    nwin = 3 * BLOCK
    nkeys = nwin + N_META
    ncols = heads * BLOCK
    gcols = group * BLOCK
    k_pad = -(-nkeys // (2 * LANES)) * (2 * LANES)

    @pl.when((b == 0) & (step == 0))
    def _build_bias():
        kr = lax.broadcasted_iota(jnp.int32, (nkeys, ncols), 0)
        col = lax.broadcasted_iota(jnp.int32, (nkeys, ncols), 1)
        dt = jnp.abs((col & (BLOCK - 1)) - (kr - BLOCK))
        head = (col >> _BLOCK_BITS).astype(F32)
        slope = jnp.exp2(-(head + 1.0) * (8.0 / heads))
        base = jnp.where((kr < nwin) & (dt <= WINDOW), dt.astype(F32) * (-LOG2E) * slope,
                         jnp.where(kr >= nwin, 0.0, NEG))
        bias_ref[0] = base
        bias_ref[1] = jnp.where(kr < BLOCK, NEG, base)
        bias_ref[2] = jnp.where((kr >= 2 * BLOCK) & (kr < nwin), NEG, base)

    col_head = lax.broadcasted_iota(jnp.int32, (1, ncols), 1) >> _BLOCK_BITS
    sink_row = jnp.zeros((1, ncols), F32)
    for hd in range(heads):
        sink_row = jnp.where(col_head == hd, sink_ref[hd] * LOG2E, sink_row)

    lane = lax.broadcasted_iota(jnp.int32, (BLOCK, LANES), 1)
    lo = lane < HEAD_DIM
    zero = jnp.zeros((BLOCK, LANES), BF16)
    k_blocks = ([kprev_ref[...]]
                + [kcur_ref[j * BLOCK:(j + 1) * BLOCK, :] for j in range(nblk)] + [knext_ref[...]])
    vt_blocks = ([vprev_ref[0]]
                 + [vcur_ref[0, :, j * BLOCK:(j + 1) * BLOCK] for j in range(nblk)] + [vnext_ref[0]])
    k_meta = km_ref[:N_META, :]
    vt_meta = vtm_ref[...]

    all_scores = []
    for blk in range(nblk):
        i = step * nblk + blk
        k_win = jnp.concatenate(k_blocks[blk:blk + 3] + [k_meta], axis=0)

        q = q_ref[blk * BLOCK:(blk + 1) * BLOCK, :]
        q_heads = []
        for hd in range(heads):
            part = q[:, (hd // 2) * LANES:(hd // 2 + 1) * LANES]
            in_lo, want_lo = hd % 2 == 0, (hd // group) % 2 == 0
            if in_lo != want_lo:
                part = jnp.concatenate([part[:, HEAD_DIM:], part[:, :HEAD_DIM]], axis=1)
            q_heads.append(jnp.where(lo if want_lo else ~lo, part, zero))
        q_all = jnp.concatenate(q_heads, axis=0)

        variant = jnp.where(i == 0, 1, jnp.where(i == nb - 1, 2, 0))
        all_scores.append(
            lax.dot_general(k_win, q_all, _DN_NT, preferred_element_type=F32) + bias_ref[variant])

    for blk, s in enumerate(all_scores):
        vt_win = jnp.concatenate(vt_blocks[blk:blk + 3] + [vt_meta], axis=1)
        m = jnp.maximum(jnp.max(s, axis=0, keepdims=True), sink_row)
        p = jnp.exp2(s - m).astype(BF16)
        p = jnp.concatenate([p, jnp.zeros((k_pad - nkeys, ncols), BF16)], axis=0)
        sink_term = jnp.exp2(sink_row - m)

        outs = []
        for g in range(kv_heads):
            cols = slice(g * gcols, (g + 1) * gcols)
            og = jnp.dot(vt_win[g * B_VT_ROWS:(g + 1) * B_VT_ROWS], p[:, cols],
                         preferred_element_type=F32)
            outs.append(og[:HEAD_DIM] / (og[HEAD_DIM:HEAD_DIM + 1] + sink_term[:, cols]))
        pairs = []
        for j in range(heads // 2):
            g, first = (2 * j) // group, (2 * j) % group
            pair = outs[g][:, first * BLOCK:(first + 2) * BLOCK]
            pairs.append(jnp.concatenate([pair[:, :BLOCK], pair[:, BLOCK:]], axis=0).T)
        o_ref[blk * BLOCK:(blk + 1) * BLOCK, :] = jnp.concatenate(pairs, axis=1).astype(BF16)


def _attn_b(pb, vtb, pb_meta, vtb_meta, sink, batch, seq, heads, kv_heads, nblk, slab):
    nb = seq // BLOCK
    n_steps = nb // nblk
    qw = heads * HEAD_DIM
    kcol = qw // LANES
    rows = kv_heads * B_VT_ROWS
    per_slab = slab // BLOCK
    assert kv_heads * HEAD_DIM == LANES and nb % nblk == 0 and per_slab % nblk == 0
    prev_blk = lambda i: jnp.maximum(nblk * i - 1, 0)
    next_blk = lambda i: jnp.minimum(nblk * i + nblk, nb - 1)
    vt_spec = lambda blk_of: pl.BlockSpec(
        (1, rows, BLOCK),
        lambda b, i: (b * (nb // per_slab) + blk_of(i) // per_slab, 0, blk_of(i) % per_slab))
    return pl.pallas_call(
        functools.partial(_attn_b_kernel, seq=seq, heads=heads, group=heads // kv_heads, nblk=nblk),
        grid=(batch, n_steps),
        in_specs=[
            pl.BlockSpec(memory_space=pltpu.SMEM),
            pl.BlockSpec((nblk * BLOCK, qw), lambda b, i: (b * n_steps + i, 0)),
            pl.BlockSpec((BLOCK, LANES), lambda b, i: (b * nb + prev_blk(i), kcol)),
            pl.BlockSpec((nblk * BLOCK, LANES), lambda b, i: (b * n_steps + i, kcol)),
            pl.BlockSpec((BLOCK, LANES), lambda b, i: (b * nb + next_blk(i), kcol)),
            vt_spec(prev_blk),
            pl.BlockSpec((1, rows, nblk * BLOCK),
                         lambda b, i: (b * (nb // per_slab) + (nblk * i) // per_slab, 0,
                                       i % (per_slab // nblk))),
            vt_spec(next_blk),
            _resident((META_PAD, LANES), lambda b, i: (0, kcol)),
            _resident((rows, META_PAD), lambda b, i: (0, 0)),
        ],
        out_specs=pl.BlockSpec((nblk * BLOCK, qw), lambda b, i: (b * n_steps + i, 0)),
        out_shape=jax.ShapeDtypeStruct((batch * seq, qw), BF16),
        scratch_shapes=[pltpu.VMEM((3, 3 * BLOCK + N_META, heads * BLOCK), F32)],
        compiler_params=pltpu.CompilerParams(
            dimension_semantics=("arbitrary", "arbitrary"),
            vmem_limit_bytes=VMEM_LIMIT_BYTES),
        name="attn_b",
    )(sink, pb, pb, pb, pb, vtb, vtb, vtb, pb_meta, vtb_meta)


def _rms(x, g):
    ms = jnp.mean(x * x, axis=-1, keepdims=True)
    return (x * lax.rsqrt(ms + EPS)) * g


def _post_kernel(x_ref, ya_ref, yb_ref, gate_ref, wba_ref, wbb_ref, wo_ref, nf_ref,
                 wg_ref, wu_ref, wd_ref, nfin_ref, o_ref, *, d_model):
    dot = functools.partial(jnp.dot, preferred_element_type=F32)
    rows = x_ref.shape[0] // POST_ROW_GROUPS
    groups = [slice(r * rows, (r + 1) * rows) for r in range(POST_ROW_GROUPS)]
    h1 = []
    for r in groups:
        ga = gate_ref[r, :d_model].astype(F32)
        gb = gate_ref[r, d_model:].astype(F32)
        merged = ga * dot(ya_ref[r, :], wba_ref[...]) + gb * dot(yb_ref[r, :], wbb_ref[...])
        h1.append(x_ref[r, :] + dot(merged.astype(BF16), wo_ref[...]))
    hn = [_rms(h, nf_ref[...]).astype(BF16) for h in h1]
    d_ff = wg_ref.shape[1]
    bounds = [min(k * FFN_COLUMN_GROUP, d_ff) for k in range(-(-d_ff // FFN_COLUMN_GROUP) + 1)]
    h2 = list(h1)
    for lo, hi in zip(bounds[:-1], bounds[1:]):
        for i in range(POST_ROW_GROUPS):
            act = jax.nn.silu(dot(hn[i], wg_ref[:, lo:hi])) * dot(hn[i], wu_ref[:, lo:hi])
            h2[i] = h2[i] + dot(act.astype(BF16), wd_ref[lo:hi, :])
    for r, h in zip(groups, h2):
        o_ref[r, :] = _rms(h, nfin_ref[...])


def _post(x2d, ya, yb, gates, wba, wbb, wo, nf, wg, wu, wd, nfin, tm):
    m, d = x2d.shape
    row = lambda w: pl.BlockSpec((tm, w), lambda i: (i, 0))
    full = lambda a: _resident(a.shape, lambda i: (0, 0))
    return pl.pallas_call(
        functools.partial(_post_kernel, d_model=d),
        grid=(m // tm,),
        in_specs=[row(d), row(ya.shape[1]), row(yb.shape[1]), row(2 * d),
                  full(wba), full(wbb), full(wo), full(nf),
                  full(wg), full(wu), full(wd), full(nfin)],
        out_specs=row(d),
        out_shape=jax.ShapeDtypeStruct((m, d), F32),
        compiler_params=pltpu.CompilerParams(
            dimension_semantics=("arbitrary",), vmem_limit_bytes=VMEM_LIMIT_BYTES),
        name="post_ffn",
    )(x2d, ya, yb, gates, wba, wbb, wo, nf, wg, wu, wd, nfin)


def _extend_heads(vt_heads, first_row):
    heads, dim, n = vt_heads.shape
    extra = jnp.zeros((heads, BF16_SUBLANES, n), vt_heads.dtype).at[:, 0, :].set(first_row)
    return jnp.concatenate([vt_heads, extra], axis=1).reshape(heads * (dim + BF16_SUBLANES), n)


def kernel(x, meta_tokens, norm_mix, w_in, lambda_q1, lambda_k1, lambda_q2, lambda_k2,
           subln_gain, sink_logits, w_branch_a, w_branch_b, w_out, norm_ffn,
           w_ff_gate, w_ff_up, w_ff_down, norm_final):
    batch, seq, d = x.shape
    assert norm_mix.shape[0] == 1, "single-layer block"
    a_heads = w_branch_a.shape[1] // A_V_DIM
    b_heads = sink_logits.shape[1]
    b_kv = max(1, b_heads // 4)
    qa_w = a_heads * 2 * HEAD_DIM
    qk_w = 2 * qa_w
    va_w = a_heads * A_V_DIM
    qb_w = b_heads * HEAD_DIM
    kvb_w = b_kv * HEAD_DIM
    pb_lo = qk_w + va_w
    vb_lo = pb_lo + qb_w + kvb_w
    gates_from = vb_lo + kvb_w

    x2d = x.reshape(batch * seq, d)
    col_scale = (jnp.ones((w_in.shape[2],), F32)
                 .at[:qa_w].set(SCALE * LOG2E).at[pb_lo:pb_lo + qb_w].set(SCALE * LOG2E))
    w_in_b = (w_in[0] * col_scale).astype(BF16)
    wvt_a = _extend_heads(w_in_b[:, qk_w:pb_lo].T.reshape(a_heads, A_V_DIM, d), 0)
    wvt_b = _extend_heads(w_in_b[:, vb_lo:gates_from].T.reshape(b_kv, HEAD_DIM, d), 0)
    wvt = jnp.concatenate([wvt_a, wvt_b], axis=0)
    na, nbt = wvt_a.shape[0], wvt_b.shape[0]
    ones_rows = (tuple(range(A_V_DIM, na, A_VT_ROWS))
                 + tuple(range(na + HEAD_DIM, na + nbt, B_VT_ROWS)))
    gain = norm_mix[0][None, :]

    pa, pb, vta, vtb, gates = _in_proj(
        x2d, gain, w_in_b, wvt, ((0, qk_w), (pb_lo, vb_lo)), gates_from,
        ((0, na, A_KEY_CHUNK), (na, na + nbt, IN_PROJ_ROW_TILE)), ones_rows, tm=IN_PROJ_ROW_TILE)
    pa_m, va_m, pb_m, vb_m = _in_proj(
        meta_tokens.astype(F32), gain, w_in_b, wvt,
        ((0, qk_w), (qk_w, pb_lo), (pb_lo, vb_lo), (vb_lo, gates_from)), None, (), (), tm=N_META)
    pad = ((0, META_PAD - N_META), (0, 0))
    pa_m = jnp.pad(pa_m, pad)
    pb_m = jnp.pad(pb_m, pad)
    vta_m = _extend_heads(jnp.pad(va_m, pad).T.reshape(a_heads, A_V_DIM, META_PAD), 1)
    vtb_m = _extend_heads(jnp.pad(vb_m, pad).T.reshape(b_kv, HEAD_DIM, META_PAD), 1)

    slopes = jnp.asarray([2.0 ** (-8.0 * (i + 1) / a_heads) for i in range(a_heads)], F32)
    ya = _attn_a(pa, vta, pa_m, vta_m, slopes, lambda_q1, lambda_k1, lambda_q2, lambda_k2,
                 subln_gain[0][:, None], batch, seq, a_heads,
                 tq=A_QUERY_TILE, tk=A_KEY_CHUNK)
    yb = _attn_b(pb, vtb, pb_m, vtb_m, sink_logits[0], batch, seq, b_heads, b_kv,
                 nblk=B_BLOCKS_PER_STEP, slab=IN_PROJ_ROW_TILE)

    out = _post(x2d, ya, yb, gates,
                w_branch_a[0].astype(BF16), w_branch_b[0].astype(BF16), w_out[0].astype(BF16),
                norm_ffn[0][None, :], w_ff_gate[0].astype(BF16), w_ff_up[0].astype(BF16),
                w_ff_down[0].astype(BF16), norm_final[None, :], tm=POST_ROW_TILE)
    return out.reshape(batch, seq, d)
```

```python
import functools
import math

import ml_dtypes
import numpy as np

import jax
import jax.numpy as jnp
from jax import lax
from jax.experimental import pallas as pl
from jax.experimental.pallas import tpu as pltpu

F32 = jnp.float32
BF16 = jnp.bfloat16

N_META = 16
HEAD_DIM = 64
WINDOW = 128
BLOCK = 128
A_V_DIM = 2 * HEAD_DIM
EPS = 1e-6
LAMBDA_INIT = 0.8 - 0.6 * math.exp(-0.3 * 0)
SCALE = HEAD_DIM ** -0.5
LOG2E = math.log2(math.e)

LANES = 128
BF16_SUBLANES = 16
META_PAD = 128
NEG = -1e30
VMEM_LIMIT_BYTES = 56 * 1024 * 1024

IN_PROJ_ROW_TILE = 1024
A_KEY_CHUNK = 512
A_QUERY_TILE = 1024
A_SCORES_AHEAD = 2
POST_ROW_TILE = 512
POST_ROW_GROUPS = 2
FFN_COLUMN_GROUP = 1536

A_VT_ROWS = A_V_DIM + BF16_SUBLANES
B_VT_ROWS = HEAD_DIM + BF16_SUBLANES
B_BLOCKS_PER_STEP = 8
_BLOCK_BITS = BLOCK.bit_length() - 1

_DN_NT = (((1,), (1,)), ((), ()))


def _resident(block_shape, index_map):
    return pl.BlockSpec(block_shape, index_map, pipeline_mode=pl.Buffered(1))


def _in_proj_kernel(x_ref, g_ref, w_ref, wvt_ref, *out_refs, splits, gates_from, vt_splits,
                    ones_rows):
    x = x_ref[...]
    ms = jnp.mean(x * x, axis=-1, keepdims=True)
    hn = ((x * lax.rsqrt(ms + EPS)) * g_ref[...]).astype(BF16)
    if gates_from is not None:
        g = jnp.dot(hn, w_ref[:, gates_from:], preferred_element_type=F32)
        out_refs[-1][...] = jax.nn.sigmoid(g).astype(BF16)
    for (lo, hi), o_ref in zip(splits, out_refs):
        o_ref[...] = jnp.dot(hn, w_ref[:, lo:hi], preferred_element_type=F32).astype(BF16)
    if gates_from is not None:
        vt_refs = out_refs[len(splits):-1]
        vt = lax.dot_general(wvt_ref[...], hn, _DN_NT, preferred_element_type=F32)
        row = lax.broadcasted_iota(jnp.int32, vt.shape, 0)
        is_ones = functools.reduce(jnp.logical_or, [row == r for r in ones_rows])
        vt = jnp.where(is_ones, 1.0, vt).astype(BF16)
        for (lo, hi, slab), vt_ref in zip(vt_splits, vt_refs):
            for j in range(vt.shape[1] // slab):
                vt_ref[j] = vt[lo:hi, j * slab:(j + 1) * slab]


def _in_proj(x2d, gain, w_bf16, wvt_bf16, splits, gates_from, vt_splits, ones_rows, tm):
    m, d = x2d.shape
    n_all = w_bf16.shape[1]
    nv = wvt_bf16.shape[0]
    widths = [hi - lo for lo, hi in splits]
    out_shape = [jax.ShapeDtypeStruct((m, w), BF16) for w in widths]
    out_specs = [pl.BlockSpec((tm, w), lambda i: (i, 0)) for w in widths]
    if gates_from is not None:
        for lo, hi, slab in vt_splits:
            out_shape.append(jax.ShapeDtypeStruct((m // slab, hi - lo, slab), BF16))
            out_specs.append(pl.BlockSpec((tm // slab, hi - lo, slab), lambda i: (i, 0, 0)))
        out_shape.append(jax.ShapeDtypeStruct((m, n_all - gates_from), BF16))
        out_specs.append(pl.BlockSpec((tm, n_all - gates_from), lambda i: (i, 0)))
    return pl.pallas_call(
        functools.partial(_in_proj_kernel, splits=splits, gates_from=gates_from,
                          vt_splits=vt_splits, ones_rows=ones_rows),
        grid=(m // tm,),
        in_specs=[
            pl.BlockSpec((tm, d), lambda i: (i, 0)),
            _resident((1, d), lambda i: (0, 0)),
            _resident((d, n_all), lambda i: (0, 0)),
            _resident((nv, d), lambda i: (0, 0)),
        ],
        out_specs=out_specs,
        out_shape=out_shape,
        compiler_params=pltpu.CompilerParams(
            dimension_semantics=("arbitrary",), vmem_limit_bytes=VMEM_LIMIT_BYTES),
        name="in_proj_meta" if gates_from is None else "in_proj_gates",
    )(x2d, gain, w_bf16, wvt_bf16)


_FEAT_SPLIT_BITS = 6
_N_PIECES = 3


def _bf16_pieces(x, n):
    pieces, rest = [], np.float32(x)
    for _ in range(n):
        piece = np.float32(np.asarray(rest, dtype=ml_dtypes.bfloat16))
        pieces.append(piece)
        rest = np.float32(rest - piece)
    return pieces


def _feature_table():
    e = _bf16_pieces(LOG2E, _N_PIECES)
    t = np.zeros((8, LANES), np.float32)
    n = _N_PIECES
    t[0, 0:n] = 1.0
    t[1, n:2 * n] = 1.0
    t[2, 2 * n:3 * n] = e
    t[2, 3 * n:4 * n] = e
    t[3, 2 * n:3 * n] = 1.0
    t[4, 3 * n:4 * n] = 1.0
    t[5, 0:n] = e
    t[5, n:2 * n] = e
    return jnp.asarray(t)


def _pos_terms(pos, slope):
    hi = (pos >> _FEAT_SPLIT_BITS).astype(F32) * (slope * (1 << _FEAT_SPLIT_BITS))
    lo = (pos & ((1 << _FEAT_SPLIT_BITS) - 1)).astype(F32) * slope
    return hi, lo


def _attn_a_kernel(slopes_ref, lq1_ref, lk1_ref, lq2_ref, lk2_ref, gain_ref, feat_ref,
                   q_ref, k_ref, vt_ref, km_ref, vtm_ref, o_ref,
                   kaug_ref, qfeat_ref, negabs_ref, *, tq, tk, seq):
    b = pl.program_id(0)
    h = pl.program_id(1)
    qi = pl.program_id(2)
    slope = slopes_ref[h]
    nqc = 2 * tq
    n_chunks = seq // tk
    n_off = max(1, tk // tq)
    n_diag = max(1, tq // tk)

    @pl.when((b == 0) & (h == 0) & (qi == 0))
    def _build_distance_tiles():
        for v in range(max(n_off, n_diag)):
            start = v * tq if tk >= tq else -v * tk
            d = ((lax.broadcasted_iota(jnp.int32, (tk, nqc), 1) & (tq - 1))
                 - lax.broadcasted_iota(jnp.int32, (tk, nqc), 0) + start)
            negabs_ref[v] = -jnp.abs(d).astype(F32)

    @pl.when(qi == 0)
    def _build_features():
        def fill(c, carry):
            rows = pl.ds(pl.multiple_of(c * tk, tk), tk)
            pos = c * tk + lax.broadcasted_iota(jnp.int32, (tk, LANES), 0)
            hi, lo = _pos_terms(pos, slope)
            kaug_ref[rows, :LANES] = k_ref[rows, :]
            kf = hi * feat_ref[0:1, :] + lo * feat_ref[1:2, :] + feat_ref[2:3, :]
            kaug_ref[rows, LANES:] = kf.astype(BF16)
            qf = feat_ref[5:6, :] - hi * feat_ref[3:4, :] - lo * feat_ref[4:5, :]
            qfeat_ref[rows, :] = qf.astype(BF16)
            return carry
        lax.fori_loop(0, n_chunks, fill, 0)

    q = q_ref[...]
    lane = lax.broadcasted_iota(jnp.int32, (tq, LANES), 1)
    zero = jnp.zeros_like(q)
    qq = jnp.concatenate([jnp.where(lane < HEAD_DIM, q, zero),
                          jnp.where(lane >= HEAD_DIM, q, zero)], axis=0)
    qf = qfeat_ref[pl.ds(pl.multiple_of(qi * tq, tq), tq), :]
    qf = jnp.concatenate([qf, qf], axis=0)
    qq_t, qf_t = qq.T, qf.T
    q_left = jnp.concatenate([qq_t, qf_t], axis=0)
    q_right = jnp.concatenate([qq_t, -qf_t], axis=0)
    q_plain = jnp.concatenate([qq_t, jnp.zeros_like(qf_t)], axis=0)

    def scores(k_rows, q_aug_t):
        return jnp.dot(k_rows, q_aug_t, preferred_element_type=F32)

    c_diag = (qi * tq) // tk

    def chunk_of(t):
        if t < n_diag:
            return c_diag + t
        u = t - n_diag
        return u + (u >= c_diag).astype(jnp.int32) * n_diag

    def chunk_scores(t):
        c = chunk_of(t)
        k_rows = kaug_ref[pl.ds(pl.multiple_of(c * tk, tk), tk), :]
        if t < n_diag:
            variant = qi - c_diag * n_off if tk >= tq else t
            return scores(k_rows, q_plain) + negabs_ref[variant] * (slope * LOG2E)
        return scores(k_rows, jnp.where(c < c_diag, q_left, q_right))

    def softmax(s, m):
        m_new = jnp.maximum(m, jnp.max(s, axis=0, keepdims=True))
        return m_new, jnp.exp2(s - m_new).astype(BF16), jnp.exp2(m - m_new)

    km_aug = jnp.concatenate([km_ref[...], jnp.zeros((META_PAD, LANES), BF16)], axis=1)
    meta_valid = lax.broadcasted_iota(jnp.int32, (META_PAD, nqc), 0) < N_META
    s = jnp.where(meta_valid, scores(km_aug, q_plain), NEG)
    m = jnp.max(s, axis=0, keepdims=True)
    acc = jnp.dot(vtm_ref[...], jnp.exp2(s - m).astype(BF16), preferred_element_type=F32)

    pending = [chunk_scores(t) for t in range(min(A_SCORES_AHEAD, n_chunks))]
    for t in range(n_chunks):
        if t + A_SCORES_AHEAD < n_chunks:
            pending.append(chunk_scores(t + A_SCORES_AHEAD))
        m, p, alpha = softmax(pending.pop(0), m)
        acc = alpha * acc + jnp.dot(vt_ref[chunk_of(t)], p, preferred_element_type=F32)

    lam = (jnp.exp(jnp.sum(lq1_ref[...] * lk1_ref[...], axis=-1, keepdims=True))
           - jnp.exp(jnp.sum(lq2_ref[...] * lk2_ref[...], axis=-1, keepdims=True))
           + LAMBDA_INIT)
    o = acc[:A_V_DIM] / acc[A_V_DIM:A_V_DIM + 1]
    y = o[:, :tq] - lam * o[:, tq:]
    ms = jnp.mean(y * y, axis=0, keepdims=True)
    y = ((y * lax.rsqrt(ms + EPS)) * gain_ref[...]) * (1.0 - LAMBDA_INIT)
    o_ref[...] = y.T.astype(BF16)


def _attn_a(pa, vt, pa_meta, vt_meta, slopes, lq1, lk1, lq2, lk2, gain_col,
            batch, seq, heads, tq, tk):
    nq = seq // tq
    width = heads * A_V_DIM
    assert max(tk, tq) % min(tk, tq) == 0 and seq % tk == 0 and seq % tq == 0
    vec = lambda n: _resident((1, n), lambda b, h, i: (0, 0))
    return pl.pallas_call(
        functools.partial(_attn_a_kernel, tq=tq, tk=tk, seq=seq),
        grid=(batch, heads, nq),
        in_specs=[
            pl.BlockSpec(memory_space=pltpu.SMEM),
            vec(HEAD_DIM), vec(HEAD_DIM), vec(HEAD_DIM), vec(HEAD_DIM),
            _resident((A_V_DIM, 1), lambda b, h, i: (0, 0)),
            _resident((8, LANES), lambda b, h, i: (0, 0)),
            pl.BlockSpec((tq, LANES), lambda b, h, i: (b * nq + i, h)),
            pl.BlockSpec((seq, LANES), lambda b, h, i: (b, heads + h)),
            pl.BlockSpec((seq // tk, A_VT_ROWS, tk), lambda b, h, i: (b, h, 0)),
            pl.BlockSpec((META_PAD, LANES), lambda b, h, i: (0, heads + h)),
            pl.BlockSpec((A_VT_ROWS, META_PAD), lambda b, h, i: (h, 0)),
        ],
        out_specs=pl.BlockSpec((tq, A_V_DIM), lambda b, h, i: (b * nq + i, h)),
        out_shape=jax.ShapeDtypeStruct((batch * seq, width), BF16),
        scratch_shapes=[pltpu.VMEM((seq, 2 * LANES), BF16),
                        pltpu.VMEM((seq, LANES), BF16),
                        pltpu.VMEM((max(tk // tq, tq // tk), tk, 2 * tq), F32)],
        compiler_params=pltpu.CompilerParams(
            dimension_semantics=("arbitrary", "arbitrary", "arbitrary"),
            vmem_limit_bytes=VMEM_LIMIT_BYTES),
        name="attn_a",
    )(slopes, lq1, lk1, lq2, lk2, gain_col, _feature_table(), pa, pa, vt, pa_meta, vt_meta)


def _attn_b_kernel(sink_ref, q_ref, kprev_ref, kcur_ref, knext_ref, vprev_ref, vcur_ref,
                   vnext_ref, km_ref, vtm_ref, o_ref, bias_ref, *, seq, heads, group, nblk):
    b = pl.program_id(0)
    step = pl.program_id(1)
    nb = seq // BLOCK
    kv_heads = heads // group
    nwin = 3 * BLOCK
    nkeys = nwin + N_META
    ncols = heads * BLOCK
    gcols = group * BLOCK
    k_pad = -(-nkeys // (2 * LANES)) * (2 * LANES)

    @pl.when((b == 0) & (step == 0))
    def _build_bias():
        kr = lax.broadcasted_iota(jnp.int32, (nkeys, ncols), 0)
        col = lax.broadcasted_iota(jnp.int32, (nkeys, ncols), 1)
        dt = jnp.abs((col & (BLOCK - 1)) - (kr - BLOCK))
        head = (col >> _BLOCK_BITS).astype(F32)
        slope = jnp.exp2(-(head + 1.0) * (8.0 / heads))
        base = jnp.where((kr < nwin) & (dt <= WINDOW), dt.astype(F32) * (-LOG2E) * slope,
                         jnp.where(kr >= nwin, 0.0, NEG))
        bias_ref[0] = base
        bias_ref[1] = jnp.where(kr < BLOCK, NEG, base)
        bias_ref[2] = jnp.where((kr >= 2 * BLOCK) & (kr < nwin), NEG, base)

    col_head = lax.broadcasted_iota(jnp.int32, (1, ncols), 1) >> _BLOCK_BITS
    sink_row = jnp.zeros((1, ncols), F32)
    for hd in range(heads):
        sink_row = jnp.where(col_head == hd, sink_ref[hd] * LOG2E, sink_row)

    lane = lax.broadcasted_iota(jnp.int32, (BLOCK, LANES), 1)
    lo = lane < HEAD_DIM
    zero = jnp.zeros((BLOCK, LANES), BF16)
    k_blocks = ([kprev_ref[...]]
                + [kcur_ref[j * BLOCK:(j + 1) * BLOCK, :] for j in range(nblk)] + [knext_ref[...]])
    vt_blocks = ([vprev_ref[0]]
                 + [vcur_ref[0, :, j * BLOCK:(j + 1) * BLOCK] for j in range(nblk)] + [vnext_ref[0]])
    k_meta = km_ref[:N_META, :]
    vt_meta = vtm_ref[...]

    all_scores = []
    for blk in range(nblk):
        i = step * nblk + blk
        k_win = jnp.concatenate(k_blocks[blk:blk + 3] + [k_meta], axis=0)

        q = q_ref[blk * BLOCK:(blk + 1) * BLOCK, :]
        q_heads = []
        for hd in range(heads):
            part = q[:, (hd // 2) * LANES:(hd // 2 + 1) * LANES]
            in_lo, want_lo = hd % 2 == 0, (hd // group) % 2 == 0
            if in_lo != want_lo:
                part = jnp.concatenate([part[:, HEAD_DIM:], part[:, :HEAD_DIM]], axis=1)
            q_heads.append(jnp.where(lo if want_lo else ~lo, part, zero))
        q_all = jnp.concatenate(q_heads, axis=0)

        variant = jnp.where(i == 0, 1, jnp.where(i == nb - 1, 2, 0))
        all_scores.append(
            lax.dot_general(k_win, q_all, _DN_NT, preferred_element_type=F32) + bias_ref[variant])

    for blk, s in enumerate(all_scores):
        vt_win = jnp.concatenate(vt_blocks[blk:blk + 3] + [vt_meta], axis=1)
        m = jnp.maximum(jnp.max(s, axis=0, keepdims=True), sink_row)
        p = jnp.exp2(s - m).astype(BF16)
        p = jnp.concatenate([p, jnp.zeros((k_pad - nkeys, ncols), BF16)], axis=0)
        sink_term = jnp.exp2(sink_row - m)

        outs = []
        for g in range(kv_heads):
            cols = slice(g * gcols, (g + 1) * gcols)
            og = jnp.dot(vt_win[g * B_VT_ROWS:(g + 1) * B_VT_ROWS], p[:, cols],
                         preferred_element_type=F32)
            outs.append(og[:HEAD_DIM] / (og[HEAD_DIM:HEAD_DIM + 1] + sink_term[:, cols]))
        pairs = []
        for j in range(heads // 2):
            g, first = (2 * j) // group, (2 * j) % group
            pair = outs[g][:, first * BLOCK:(first + 2) * BLOCK]
            pairs.append(jnp.concatenate([pair[:, :BLOCK], pair[:, BLOCK:]], axis=0).T)
        o_ref[blk * BLOCK:(blk + 1) * BLOCK, :] = jnp.concatenate(pairs, axis=1).astype(BF16)


def _attn_b(pb, vtb, pb_meta, vtb_meta, sink, batch, seq, heads, kv_heads, nblk, slab):
    nb = seq // BLOCK
    n_steps = nb // nblk
    qw = heads * HEAD_DIM
    kcol = qw // LANES
    rows = kv_heads * B_VT_ROWS
    per_slab = slab // BLOCK
    assert kv_heads * HEAD_DIM == LANES and nb % nblk == 0 and per_slab % nblk == 0
    prev_blk = lambda i: jnp.maximum(nblk * i - 1, 0)
    next_blk = lambda i: jnp.minimum(nblk * i + nblk, nb - 1)
    vt_spec = lambda blk_of: pl.BlockSpec(
        (1, rows, BLOCK),
        lambda b, i: (b * (nb // per_slab) + blk_of(i) // per_slab, 0, blk_of(i) % per_slab))
    return pl.pallas_call(
        functools.partial(_attn_b_kernel, seq=seq, heads=heads, group=heads // kv_heads, nblk=nblk),
        grid=(batch, n_steps),
        in_specs=[
            pl.BlockSpec(memory_space=pltpu.SMEM),
            pl.BlockSpec((nblk * BLOCK, qw), lambda b, i: (b * n_steps + i, 0)),
            pl.BlockSpec((BLOCK, LANES), lambda b, i: (b * nb + prev_blk(i), kcol)),
            pl.BlockSpec((nblk * BLOCK, LANES), lambda b, i: (b * n_steps + i, kcol)),
            pl.BlockSpec((BLOCK, LANES), lambda b, i: (b * nb + next_blk(i), kcol)),
            vt_spec(prev_blk),
            pl.BlockSpec((1, rows, nblk * BLOCK),
                         lambda b, i: (b * (nb // per_slab) + (nblk * i) // per_slab, 0,
                                       i % (per_slab // nblk))),
            vt_spec(next_blk),
            _resident((META_PAD, LANES), lambda b, i: (0, kcol)),
            _resident((rows, META_PAD), lambda b, i: (0, 0)),
        ],
        out_specs=pl.BlockSpec((nblk * BLOCK, qw), lambda b, i: (b * n_steps + i, 0)),
        out_shape=jax.ShapeDtypeStruct((batch * seq, qw), BF16),
        scratch_shapes=[pltpu.VMEM((3, 3 * BLOCK + N_META, heads * BLOCK), F32)],
        compiler_params=pltpu.CompilerParams(
            dimension_semantics=("arbitrary", "arbitrary"),
            vmem_limit_bytes=VMEM_LIMIT_BYTES),
        name="attn_b",
    )(sink, pb, pb, pb, pb, vtb, vtb, vtb, pb_meta, vtb_meta)


def _rms(x, g):
    ms = jnp.mean(x * x, axis=-1, keepdims=True)
    return (x * lax.rsqrt(ms + EPS)) * g


def _post_kernel(x_ref, ya_ref, yb_ref, gate_ref, wba_ref, wbb_ref, wo_ref, nf_ref,
                 wg_ref, wu_ref, wd_ref, nfin_ref, o_ref, *, d_model):
    dot = functools.partial(jnp.dot, preferred_element_type=F32)
    rows = x_ref.shape[0] // POST_ROW_GROUPS
    groups = [slice(r * rows, (r + 1) * rows) for r in range(POST_ROW_GROUPS)]
    h1 = []
    for r in groups:
        ga = gate_ref[r, :d_model].astype(F32)
        gb = gate_ref[r, d_model:].astype(F32)
        merged = ga * dot(ya_ref[r, :], wba_ref[...]) + gb * dot(yb_ref[r, :], wbb_ref[...])
        h1.append(x_ref[r, :] + dot(merged.astype(BF16), wo_ref[...]))
    hn = [_rms(h, nf_ref[...]).astype(BF16) for h in h1]
    d_ff = wg_ref.shape[1]
    bounds = [min(k * FFN_COLUMN_GROUP, d_ff) for k in range(-(-d_ff // FFN_COLUMN_GROUP) + 1)]
    h2 = list(h1)
    for lo, hi in zip(bounds[:-1], bounds[1:]):
        for i in range(POST_ROW_GROUPS):
            act = jax.nn.silu(dot(hn[i], wg_ref[:, lo:hi])) * dot(hn[i], wu_ref[:, lo:hi])
            h2[i] = h2[i] + dot(act.astype(BF16), wd_ref[lo:hi, :])
    for r, h in zip(groups, h2):
        o_ref[r, :] = _rms(h, nfin_ref[...])


def _post(x2d, ya, yb, gates, wba, wbb, wo, nf, wg, wu, wd, nfin, tm):
    m, d = x2d.shape
    row = lambda w: pl.BlockSpec((tm, w), lambda i: (i, 0))
    full = lambda a: _resident(a.shape, lambda i: (0, 0))
    return pl.pallas_call(
        functools.partial(_post_kernel, d_model=d),
        grid=(m // tm,),
        in_specs=[row(d), row(ya.shape[1]), row(yb.shape[1]), row(2 * d),
                  full(wba), full(wbb), full(wo), full(nf),
                  full(wg), full(wu), full(wd), full(nfin)],
        out_specs=row(d),
        out_shape=jax.ShapeDtypeStruct((m, d), F32),
        compiler_params=pltpu.CompilerParams(
            dimension_semantics=("arbitrary",), vmem_limit_bytes=VMEM_LIMIT_BYTES),
        name="post_ffn",
    )(x2d, ya, yb, gates, wba, wbb, wo, nf, wg, wu, wd, nfin)


def _extend_heads(vt_heads, first_row):
    heads, dim, n = vt_heads.shape
    extra = jnp.zeros((heads, BF16_SUBLANES, n), vt_heads.dtype).at[:, 0, :].set(first_row)
    return jnp.concatenate([vt_heads, extra], axis=1).reshape(heads * (dim + BF16_SUBLANES), n)


def kernel(x, meta_tokens, norm_mix, w_in, lambda_q1, lambda_k1, lambda_q2, lambda_k2,
           subln_gain, sink_logits, w_branch_a, w_branch_b, w_out, norm_ffn,
           w_ff_gate, w_ff_up, w_ff_down, norm_final):
    batch, seq, d = x.shape
    assert norm_mix.shape[0] == 1, "single-layer block"
    a_heads = w_branch_a.shape[1] // A_V_DIM
    b_heads = sink_logits.shape[1]
    b_kv = max(1, b_heads // 4)
    qa_w = a_heads * 2 * HEAD_DIM
    qk_w = 2 * qa_w
    va_w = a_heads * A_V_DIM
    qb_w = b_heads * HEAD_DIM
    kvb_w = b_kv * HEAD_DIM
    pb_lo = qk_w + va_w
    vb_lo = pb_lo + qb_w + kvb_w
    gates_from = vb_lo + kvb_w

    x2d = x.reshape(batch * seq, d)
    col_scale = (jnp.ones((w_in.shape[2],), F32)
                 .at[:qa_w].set(SCALE * LOG2E).at[pb_lo:pb_lo + qb_w].set(SCALE * LOG2E))
    w_in_b = (w_in[0] * col_scale).astype(BF16)
    wvt_a = _extend_heads(w_in_b[:, qk_w:pb_lo].T.reshape(a_heads, A_V_DIM, d), 0)
    wvt_b = _extend_heads(w_in_b[:, vb_lo:gates_from].T.reshape(b_kv, HEAD_DIM, d), 0)
    wvt = jnp.concatenate([wvt_a, wvt_b], axis=0)
    na, nbt = wvt_a.shape[0], wvt_b.shape[0]
    ones_rows = (tuple(range(A_V_DIM, na, A_VT_ROWS))
                 + tuple(range(na + HEAD_DIM, na + nbt, B_VT_ROWS)))
    gain = norm_mix[0][None, :]

    pa, pb, vta, vtb, gates = _in_proj(
        x2d, gain, w_in_b, wvt, ((0, qk_w), (pb_lo, vb_lo)), gates_from,
        ((0, na, A_KEY_CHUNK), (na, na + nbt, IN_PROJ_ROW_TILE)), ones_rows, tm=IN_PROJ_ROW_TILE)
    pa_m, va_m, pb_m, vb_m = _in_proj(
        meta_tokens.astype(F32), gain, w_in_b, wvt,
        ((0, qk_w), (qk_w, pb_lo), (pb_lo, vb_lo), (vb_lo, gates_from)), None, (), (), tm=N_META)
    pad = ((0, META_PAD - N_META), (0, 0))
    pa_m = jnp.pad(pa_m, pad)
    pb_m = jnp.pad(pb_m, pad)
    vta_m = _extend_heads(jnp.pad(va_m, pad).T.reshape(a_heads, A_V_DIM, META_PAD), 1)
    vtb_m = _extend_heads(jnp.pad(vb_m, pad).T.reshape(b_kv, HEAD_DIM, META_PAD), 1)

    slopes = jnp.asarray([2.0 ** (-8.0 * (i + 1) / a_heads) for i in range(a_heads)], F32)
    ya = _attn_a(pa, vta, pa_m, vta_m, slopes, lambda_q1, lambda_k1, lambda_q2, lambda_k2,
                 subln_gain[0][:, None], batch, seq, a_heads,
                 tq=A_QUERY_TILE, tk=A_KEY_CHUNK)
    yb = _attn_b(pb, vtb, pb_m, vtb_m, sink_logits[0], batch, seq, b_heads, b_kv,
                 nblk=B_BLOCKS_PER_STEP, slab=IN_PROJ_ROW_TILE)

    out = _post(x2d, ya, yb, gates,
                w_branch_a[0].astype(BF16), w_branch_b[0].astype(BF16), w_out[0].astype(BF16),
                norm_ffn[0][None, :], w_ff_gate[0].astype(BF16), w_ff_up[0].astype(BF16),
                w_ff_down[0].astype(BF16), norm_final[None, :], tm=POST_ROW_TILE)
    return out.reshape(batch, seq, d)
```

```python
import functools
import math

import ml_dtypes
import numpy as np

import jax
import jax.numpy as jnp
from jax import lax
from jax.experimental import pallas as pl
from jax.experimental.pallas import tpu as pltpu

F32 = jnp.float32
BF16 = jnp.bfloat16

N_META = 16
HEAD_DIM = 64
WINDOW = 128
BLOCK = 128
A_V_DIM = 2 * HEAD_DIM
EPS = 1e-6
LAMBDA_INIT = 0.8 - 0.6 * math.exp(-0.3 * 0)
SCALE = HEAD_DIM ** -0.5
LOG2E = math.log2(math.e)

LANES = 128
BF16_SUBLANES = 16
META_PAD = 128
NEG = -1e30
VMEM_LIMIT_BYTES = 56 * 1024 * 1024

IN_PROJ_ROW_TILE = 1024
A_KEY_CHUNK = 512
A_QUERY_TILE = 1024
A_SCORES_AHEAD = 1
POST_ROW_TILE = 512
POST_ROW_GROUPS = 2
FFN_COLUMN_GROUP = 1536

A_VT_ROWS = A_V_DIM + BF16_SUBLANES
B_VT_ROWS = HEAD_DIM + BF16_SUBLANES
B_BLOCKS_PER_STEP = 8
B_SCORES_AHEAD = 2
_BLOCK_BITS = BLOCK.bit_length() - 1

_DN_NT = (((1,), (1,)), ((), ()))


def _resident(block_shape, index_map):
    return pl.BlockSpec(block_shape, index_map, pipeline_mode=pl.Buffered(1))


def _in_proj_kernel(x_ref, g_ref, w_ref, wvt_ref, *out_refs, splits, gates_from, vt_splits,
                    ones_rows):
    x = x_ref[...]
    ms = jnp.mean(x * x, axis=-1, keepdims=True)
    hn = ((x * lax.rsqrt(ms + EPS)) * g_ref[...]).astype(BF16)
    if gates_from is not None:
        g = jnp.dot(hn, w_ref[:, gates_from:], preferred_element_type=F32)
        out_refs[-1][...] = jax.nn.sigmoid(g).astype(BF16)
    for (lo, hi), o_ref in zip(splits, out_refs):
        o_ref[...] = jnp.dot(hn, w_ref[:, lo:hi], preferred_element_type=F32).astype(BF16)
    if gates_from is not None:
        vt_refs = out_refs[len(splits):-1]
        vt = lax.dot_general(wvt_ref[...], hn, _DN_NT, preferred_element_type=F32)
        row = lax.broadcasted_iota(jnp.int32, vt.shape, 0)
        is_ones = functools.reduce(jnp.logical_or, [row == r for r in ones_rows])
        vt = jnp.where(is_ones, 1.0, vt).astype(BF16)
        for (lo, hi, slab), vt_ref in zip(vt_splits, vt_refs):
            for j in range(vt.shape[1] // slab):
                vt_ref[j] = vt[lo:hi, j * slab:(j + 1) * slab]


def _in_proj(x2d, gain, w_bf16, wvt_bf16, splits, gates_from, vt_splits, ones_rows, tm):
    m, d = x2d.shape
    n_all = w_bf16.shape[1]
    nv = wvt_bf16.shape[0]
    widths = [hi - lo for lo, hi in splits]
    out_shape = [jax.ShapeDtypeStruct((m, w), BF16) for w in widths]
    out_specs = [pl.BlockSpec((tm, w), lambda i: (i, 0)) for w in widths]
    if gates_from is not None:
        for lo, hi, slab in vt_splits:
            out_shape.append(jax.ShapeDtypeStruct((m // slab, hi - lo, slab), BF16))
            out_specs.append(pl.BlockSpec((tm // slab, hi - lo, slab), lambda i: (i, 0, 0)))
        out_shape.append(jax.ShapeDtypeStruct((m, n_all - gates_from), BF16))
        out_specs.append(pl.BlockSpec((tm, n_all - gates_from), lambda i: (i, 0)))
    return pl.pallas_call(
        functools.partial(_in_proj_kernel, splits=splits, gates_from=gates_from,
                          vt_splits=vt_splits, ones_rows=ones_rows),
        grid=(m // tm,),
        in_specs=[
            pl.BlockSpec((tm, d), lambda i: (i, 0)),
            _resident((1, d), lambda i: (0, 0)),
            _resident((d, n_all), lambda i: (0, 0)),
            _resident((nv, d), lambda i: (0, 0)),
        ],
        out_specs=out_specs,
        out_shape=out_shape,
        compiler_params=pltpu.CompilerParams(
            dimension_semantics=("arbitrary",), vmem_limit_bytes=VMEM_LIMIT_BYTES),
        name="in_proj_meta" if gates_from is None else "in_proj_gates",
    )(x2d, gain, w_bf16, wvt_bf16)


_FEAT_SPLIT_BITS = 6
_N_PIECES = 3


def _bf16_pieces(x, n):
    pieces, rest = [], np.float32(x)
    for _ in range(n):
        piece = np.float32(np.asarray(rest, dtype=ml_dtypes.bfloat16))
        pieces.append(piece)
        rest = np.float32(rest - piece)
    return pieces


def _feature_table():
    e = _bf16_pieces(LOG2E, _N_PIECES)
    t = np.zeros((8, LANES), np.float32)
    n = _N_PIECES
    t[0, 0:n] = 1.0
    t[1, n:2 * n] = 1.0
    t[2, 2 * n:3 * n] = e
    t[2, 3 * n:4 * n] = e
    t[3, 2 * n:3 * n] = 1.0
    t[4, 3 * n:4 * n] = 1.0
    t[5, 0:n] = e
    t[5, n:2 * n] = e
    return jnp.asarray(t)


def _pos_terms(pos, slope):
    hi = (pos >> _FEAT_SPLIT_BITS).astype(F32) * (slope * (1 << _FEAT_SPLIT_BITS))
    lo = (pos & ((1 << _FEAT_SPLIT_BITS) - 1)).astype(F32) * slope
    return hi, lo


def _attn_a_kernel(slopes_ref, lq1_ref, lk1_ref, lq2_ref, lk2_ref, gain_ref, feat_ref,
                   q_ref, k_ref, vt_ref, km_ref, vtm_ref, o_ref,
                   kaug_ref, qfeat_ref, negabs_ref, *, tq, tk, seq):
    b = pl.program_id(0)
    h = pl.program_id(1)
    qi = pl.program_id(2)
    slope = slopes_ref[h]
    nqc = 2 * tq
    n_chunks = seq // tk
    n_off = max(1, tk // tq)
    n_diag = max(1, tq // tk)

    @pl.when((b == 0) & (h == 0) & (qi == 0))
    def _build_distance_tiles():
        for v in range(max(n_off, n_diag)):
            start = v * tq if tk >= tq else -v * tk
            d = ((lax.broadcasted_iota(jnp.int32, (tk, nqc), 1) & (tq - 1))
                 - lax.broadcasted_iota(jnp.int32, (tk, nqc), 0) + start)
            negabs_ref[v] = -jnp.abs(d).astype(F32)

    @pl.when(qi == 0)
    def _build_features():
        def fill(c, carry):
            rows = pl.ds(pl.multiple_of(c * tk, tk), tk)
            pos = c * tk + lax.broadcasted_iota(jnp.int32, (tk, LANES), 0)
            hi, lo = _pos_terms(pos, slope)
            kaug_ref[rows, :LANES] = k_ref[rows, :]
            kf = hi * feat_ref[0:1, :] + lo * feat_ref[1:2, :] + feat_ref[2:3, :]
            kaug_ref[rows, LANES:] = kf.astype(BF16)
            qf = feat_ref[5:6, :] - hi * feat_ref[3:4, :] - lo * feat_ref[4:5, :]
            qfeat_ref[rows, :] = qf.astype(BF16)
            return carry
        lax.fori_loop(0, n_chunks, fill, 0)

    q = q_ref[...]
    lane = lax.broadcasted_iota(jnp.int32, (tq, LANES), 1)
    zero = jnp.zeros_like(q)
    qq = jnp.concatenate([jnp.where(lane < HEAD_DIM, q, zero),
                          jnp.where(lane >= HEAD_DIM, q, zero)], axis=0)
    qf = qfeat_ref[pl.ds(pl.multiple_of(qi * tq, tq), tq), :]
    qf = jnp.concatenate([qf, qf], axis=0)
    qq_t, qf_t = qq.T, qf.T
    q_left = jnp.concatenate([qq_t, qf_t], axis=0)
    q_right = jnp.concatenate([qq_t, -qf_t], axis=0)
    q_plain = jnp.concatenate([qq_t, jnp.zeros_like(qf_t)], axis=0)

    def scores(k_rows, q_aug_t):
        return jnp.dot(k_rows, q_aug_t, preferred_element_type=F32)

    c_diag = (qi * tq) // tk

    def chunk_of(t):
        if t < n_diag:
            return c_diag + t
        u = t - n_diag
        return u + (u >= c_diag).astype(jnp.int32) * n_diag

    def chunk_scores(t):
        c = chunk_of(t)
        k_rows = kaug_ref[pl.ds(pl.multiple_of(c * tk, tk), tk), :]
        if t < n_diag:
            variant = qi - c_diag * n_off if tk >= tq else t
            return scores(k_rows, q_plain) + negabs_ref[variant] * (slope * LOG2E)
        return scores(k_rows, jnp.where(c < c_diag, q_left, q_right))

    def softmax(s, m):
        m_new = jnp.maximum(m, jnp.max(s, axis=0, keepdims=True))
        return m_new, jnp.exp2(s - m_new).astype(BF16), jnp.exp2(m - m_new)

    km_aug = jnp.concatenate([km_ref[...], jnp.zeros((META_PAD, LANES), BF16)], axis=1)
    meta_valid = lax.broadcasted_iota(jnp.int32, (META_PAD, nqc), 0) < N_META
    s = jnp.where(meta_valid, scores(km_aug, q_plain), NEG)
    m = jnp.max(s, axis=0, keepdims=True)
    acc = jnp.dot(vtm_ref[...], jnp.exp2(s - m).astype(BF16), preferred_element_type=F32)

    pending = [chunk_scores(t) for t in range(min(A_SCORES_AHEAD, n_chunks))]
    for t in range(n_chunks):
        if t + A_SCORES_AHEAD < n_chunks:
            pending.append(chunk_scores(t + A_SCORES_AHEAD))
        m, p, alpha = softmax(pending.pop(0), m)
        acc = alpha * acc + jnp.dot(vt_ref[chunk_of(t)], p, preferred_element_type=F32)

    lam = (jnp.exp(jnp.sum(lq1_ref[...] * lk1_ref[...], axis=-1, keepdims=True))
           - jnp.exp(jnp.sum(lq2_ref[...] * lk2_ref[...], axis=-1, keepdims=True))
           + LAMBDA_INIT)
    o = acc[:A_V_DIM] / acc[A_V_DIM:A_V_DIM + 1]
    y = o[:, :tq] - lam * o[:, tq:]
    ms = jnp.mean(y * y, axis=0, keepdims=True)
    y = ((y * lax.rsqrt(ms + EPS)) * gain_ref[...]) * (1.0 - LAMBDA_INIT)
    o_ref[...] = y.T.astype(BF16)


def _attn_a(pa, vt, pa_meta, vt_meta, slopes, lq1, lk1, lq2, lk2, gain_col,
            batch, seq, heads, tq, tk):
    nq = seq // tq
    width = heads * A_V_DIM
    assert max(tk, tq) % min(tk, tq) == 0 and seq % tk == 0 and seq % tq == 0
    vec = lambda n: _resident((1, n), lambda b, h, i: (0, 0))
    return pl.pallas_call(
        functools.partial(_attn_a_kernel, tq=tq, tk=tk, seq=seq),
        grid=(batch, heads, nq),
        in_specs=[
            pl.BlockSpec(memory_space=pltpu.SMEM),
            vec(HEAD_DIM), vec(HEAD_DIM), vec(HEAD_DIM), vec(HEAD_DIM),
            _resident((A_V_DIM, 1), lambda b, h, i: (0, 0)),
            _resident((8, LANES), lambda b, h, i: (0, 0)),
            pl.BlockSpec((tq, LANES), lambda b, h, i: (b * nq + i, h)),
            pl.BlockSpec((seq, LANES), lambda b, h, i: (b, heads + h)),
            pl.BlockSpec((seq // tk, A_VT_ROWS, tk), lambda b, h, i: (b, h, 0)),
            pl.BlockSpec((META_PAD, LANES), lambda b, h, i: (0, heads + h)),
            pl.BlockSpec((A_VT_ROWS, META_PAD), lambda b, h, i: (h, 0)),
        ],
        out_specs=pl.BlockSpec((tq, A_V_DIM), lambda b, h, i: (b * nq + i, h)),
        out_shape=jax.ShapeDtypeStruct((batch * seq, width), BF16),
        scratch_shapes=[pltpu.VMEM((seq, 2 * LANES), BF16),
                        pltpu.VMEM((seq, LANES), BF16),
                        pltpu.VMEM((max(tk // tq, tq // tk), tk, 2 * tq), F32)],
        compiler_params=pltpu.CompilerParams(
            dimension_semantics=("arbitrary", "arbitrary", "arbitrary"),
            vmem_limit_bytes=VMEM_LIMIT_BYTES),
        name="attn_a",
    )(slopes, lq1, lk1, lq2, lk2, gain_col, _feature_table(), pa, pa, vt, pa_meta, vt_meta)


def _attn_b_kernel(sink_ref, q_ref, kprev_ref, kcur_ref, knext_ref, vprev_ref, vcur_ref,
                   vnext_ref, km_ref, vtm_ref, o_ref, bias_ref, *, seq, heads, group, nblk):
    b = pl.program_id(0)
    step = pl.program_id(1)
    nb = seq // BLOCK
    kv_heads = heads // group
    nwin = 3 * BLOCK
    nkeys = nwin + N_META
    ncols = heads * BLOCK
    gcols = group * BLOCK
    k_pad = -(-nkeys // (2 * LANES)) * (2 * LANES)

    @pl.when((b == 0) & (step == 0))
    def _build_bias():
        kr = lax.broadcasted_iota(jnp.int32, (nkeys, ncols), 0)
        col = lax.broadcasted_iota(jnp.int32, (nkeys, ncols), 1)
        dt = jnp.abs((col & (BLOCK - 1)) - (kr - BLOCK))
        head = (col >> _BLOCK_BITS).astype(F32)
        slope = jnp.exp2(-(head + 1.0) * (8.0 / heads))
        base = jnp.where((kr < nwin) & (dt <= WINDOW), dt.astype(F32) * (-LOG2E) * slope,
                         jnp.where(kr >= nwin, 0.0, NEG))
        bias_ref[0] = base
        bias_ref[1] = jnp.where(kr < BLOCK, NEG, base)
        bias_ref[2] = jnp.where((kr >= 2 * BLOCK) & (kr < nwin), NEG, base)

    col_head = lax.broadcasted_iota(jnp.int32, (1, ncols), 1) >> _BLOCK_BITS
    sink_row = jnp.zeros((1, ncols), F32)
    for hd in range(heads):
        sink_row = jnp.where(col_head == hd, sink_ref[hd] * LOG2E, sink_row)

    lane = lax.broadcasted_iota(jnp.int32, (BLOCK, LANES), 1)
    lo = lane < HEAD_DIM
    zero = jnp.zeros((BLOCK, LANES), BF16)
    k_blocks = ([kprev_ref[...]]
                + [kcur_ref[j * BLOCK:(j + 1) * BLOCK, :] for j in range(nblk)] + [knext_ref[...]])
    vt_blocks = ([vprev_ref[0]]
                 + [vcur_ref[0, :, j * BLOCK:(j + 1) * BLOCK] for j in range(nblk)] + [vnext_ref[0]])
    k_meta = km_ref[:N_META, :]
    vt_meta = vtm_ref[...]

    def block_scores(blk):
        i = step * nblk + blk
        k_win = jnp.concatenate(k_blocks[blk:blk + 3] + [k_meta], axis=0)

        q = q_ref[blk * BLOCK:(blk + 1) * BLOCK, :]
        q_heads = []
        for hd in range(heads):
            part = q[:, (hd // 2) * LANES:(hd // 2 + 1) * LANES]
            in_lo, want_lo = hd % 2 == 0, (hd // group) % 2 == 0
            if in_lo != want_lo:
                part = jnp.concatenate([part[:, HEAD_DIM:], part[:, :HEAD_DIM]], axis=1)
            q_heads.append(jnp.where(lo if want_lo else ~lo, part, zero))
        q_all = jnp.concatenate(q_heads, axis=0)

        variant = jnp.where(i == 0, 1, jnp.where(i == nb - 1, 2, 0))
        return (lax.dot_general(k_win, q_all, _DN_NT, preferred_element_type=F32)
                + bias_ref[variant])

    def block_output(blk, s):
        vt_win = jnp.concatenate(vt_blocks[blk:blk + 3] + [vt_meta], axis=1)
        m = jnp.maximum(jnp.max(s, axis=0, keepdims=True), sink_row)
        p = jnp.exp2(s - m).astype(BF16)
        p = jnp.concatenate([p, jnp.zeros((k_pad - nkeys, ncols), BF16)], axis=0)
        sink_term = jnp.exp2(sink_row - m)

        outs = []
        for g in range(kv_heads):
            cols = slice(g * gcols, (g + 1) * gcols)
            og = jnp.dot(vt_win[g * B_VT_ROWS:(g + 1) * B_VT_ROWS], p[:, cols],
                         preferred_element_type=F32)
            outs.append(og[:HEAD_DIM] / (og[HEAD_DIM:HEAD_DIM + 1] + sink_term[:, cols]))
        pairs = []
        for j in range(heads // 2):
            g, first = (2 * j) // group, (2 * j) % group
            pair = outs[g][:, first * BLOCK:(first + 2) * BLOCK]
            pairs.append(jnp.concatenate([pair[:, :BLOCK], pair[:, BLOCK:]], axis=0).T)
        o_ref[blk * BLOCK:(blk + 1) * BLOCK, :] = jnp.concatenate(pairs, axis=1).astype(BF16)

    pending = [block_scores(blk) for blk in range(min(B_SCORES_AHEAD, nblk))]
    for blk in range(nblk):
        if blk + B_SCORES_AHEAD < nblk:
            pending.append(block_scores(blk + B_SCORES_AHEAD))
        block_output(blk, pending.pop(0))


def _attn_b(pb, vtb, pb_meta, vtb_meta, sink, batch, seq, heads, kv_heads, nblk, slab):
    nb = seq // BLOCK
    n_steps = nb // nblk
    qw = heads * HEAD_DIM
    kcol = qw // LANES
    rows = kv_heads * B_VT_ROWS
    per_slab = slab // BLOCK
    assert kv_heads * HEAD_DIM == LANES and nb % nblk == 0 and per_slab % nblk == 0
    prev_blk = lambda i: jnp.maximum(nblk * i - 1, 0)
    next_blk = lambda i: jnp.minimum(nblk * i + nblk, nb - 1)
    vt_spec = lambda blk_of: pl.BlockSpec(
        (1, rows, BLOCK),
        lambda b, i: (b * (nb // per_slab) + blk_of(i) // per_slab, 0, blk_of(i) % per_slab))
    return pl.pallas_call(
        functools.partial(_attn_b_kernel, seq=seq, heads=heads, group=heads // kv_heads, nblk=nblk),
        grid=(batch, n_steps),
        in_specs=[
            pl.BlockSpec(memory_space=pltpu.SMEM),
            pl.BlockSpec((nblk * BLOCK, qw), lambda b, i: (b * n_steps + i, 0)),
            pl.BlockSpec((BLOCK, LANES), lambda b, i: (b * nb + prev_blk(i), kcol)),
            pl.BlockSpec((nblk * BLOCK, LANES), lambda b, i: (b * n_steps + i, kcol)),
            pl.BlockSpec((BLOCK, LANES), lambda b, i: (b * nb + next_blk(i), kcol)),
            vt_spec(prev_blk),
            pl.BlockSpec((1, rows, nblk * BLOCK),
                         lambda b, i: (b * (nb // per_slab) + (nblk * i) // per_slab, 0,
                                       i % (per_slab // nblk))),
            vt_spec(next_blk),
            _resident((META_PAD, LANES), lambda b, i: (0, kcol)),
            _resident((rows, META_PAD), lambda b, i: (0, 0)),
        ],
        out_specs=pl.BlockSpec((nblk * BLOCK, qw), lambda b, i: (b * n_steps + i, 0)),
        out_shape=jax.ShapeDtypeStruct((batch * seq, qw), BF16),
        scratch_shapes=[pltpu.VMEM((3, 3 * BLOCK + N_META, heads * BLOCK), F32)],
        compiler_params=pltpu.CompilerParams(
            dimension_semantics=("arbitrary", "arbitrary"),
            vmem_limit_bytes=VMEM_LIMIT_BYTES),
        name="attn_b",
    )(sink, pb, pb, pb, pb, vtb, vtb, vtb, pb_meta, vtb_meta)


def _rms(x, g):
    ms = jnp.mean(x * x, axis=-1, keepdims=True)
    return (x * lax.rsqrt(ms + EPS)) * g


def _post_kernel(x_ref, ya_ref, yb_ref, gate_ref, wba_ref, wbb_ref, wo_ref, nf_ref,
                 wg_ref, wu_ref, wd_ref, nfin_ref, o_ref, *, d_model):
    dot = functools.partial(jnp.dot, preferred_element_type=F32)
    rows = x_ref.shape[0] // POST_ROW_GROUPS
    groups = [slice(r * rows, (r + 1) * rows) for r in range(POST_ROW_GROUPS)]
    h1 = []
    for r in groups:
        ga = gate_ref[r, :d_model].astype(F32)
        gb = gate_ref[r, d_model:].astype(F32)
        merged = ga * dot(ya_ref[r, :], wba_ref[...]) + gb * dot(yb_ref[r, :], wbb_ref[...])
        h1.append(x_ref[r, :] + dot(merged.astype(BF16), wo_ref[...]))
    hn = [_rms(h, nf_ref[...]).astype(BF16) for h in h1]
    d_ff = wg_ref.shape[1]
    bounds = [min(k * FFN_COLUMN_GROUP, d_ff) for k in range(-(-d_ff // FFN_COLUMN_GROUP) + 1)]
    h2 = list(h1)
    for lo, hi in zip(bounds[:-1], bounds[1:]):
        for i in range(POST_ROW_GROUPS):
            act = jax.nn.silu(dot(hn[i], wg_ref[:, lo:hi])) * dot(hn[i], wu_ref[:, lo:hi])
            h2[i] = h2[i] + dot(act.astype(BF16), wd_ref[lo:hi, :])
    for r, h in zip(groups, h2):
        o_ref[r, :] = _rms(h, nfin_ref[...])


def _post(x2d, ya, yb, gates, wba, wbb, wo, nf, wg, wu, wd, nfin, tm):
    m, d = x2d.shape
    row = lambda w: pl.BlockSpec((tm, w), lambda i: (i, 0))
    full = lambda a: _resident(a.shape, lambda i: (0, 0))
    return pl.pallas_call(
        functools.partial(_post_kernel, d_model=d),
        grid=(m // tm,),
        in_specs=[row(d), row(ya.shape[1]), row(yb.shape[1]), row(2 * d),
                  full(wba), full(wbb), full(wo), full(nf),
                  full(wg), full(wu), full(wd), full(nfin)],
        out_specs=row(d),
        out_shape=jax.ShapeDtypeStruct((m, d), F32),
        compiler_params=pltpu.CompilerParams(
            dimension_semantics=("arbitrary",), vmem_limit_bytes=VMEM_LIMIT_BYTES),
        name="post_ffn",
    )(x2d, ya, yb, gates, wba, wbb, wo, nf, wg, wu, wd, nfin)


def _extend_heads(vt_heads, first_row):
    heads, dim, n = vt_heads.shape
    extra = jnp.zeros((heads, BF16_SUBLANES, n), vt_heads.dtype).at[:, 0, :].set(first_row)
    return jnp.concatenate([vt_heads, extra], axis=1).reshape(heads * (dim + BF16_SUBLANES), n)


def kernel(x, meta_tokens, norm_mix, w_in, lambda_q1, lambda_k1, lambda_q2, lambda_k2,
           subln_gain, sink_logits, w_branch_a, w_branch_b, w_out, norm_ffn,
           w_ff_gate, w_ff_up, w_ff_down, norm_final):
    batch, seq, d = x.shape
    assert norm_mix.shape[0] == 1, "single-layer block"
    a_heads = w_branch_a.shape[1] // A_V_DIM
    b_heads = sink_logits.shape[1]
    b_kv = max(1, b_heads // 4)
    qa_w = a_heads * 2 * HEAD_DIM
    qk_w = 2 * qa_w
    va_w = a_heads * A_V_DIM
    qb_w = b_heads * HEAD_DIM
    kvb_w = b_kv * HEAD_DIM
    pb_lo = qk_w + va_w
    vb_lo = pb_lo + qb_w + kvb_w
    gates_from = vb_lo + kvb_w

    x2d = x.reshape(batch * seq, d)
    col_scale = (jnp.ones((w_in.shape[2],), F32)
                 .at[:qa_w].set(SCALE * LOG2E).at[pb_lo:pb_lo + qb_w].set(SCALE * LOG2E))
    w_in_b = (w_in[0] * col_scale).astype(BF16)
    wvt_a = _extend_heads(w_in_b[:, qk_w:pb_lo].T.reshape(a_heads, A_V_DIM, d), 0)
    wvt_b = _extend_heads(w_in_b[:, vb_lo:gates_from].T.reshape(b_kv, HEAD_DIM, d), 0)
    wvt = jnp.concatenate([wvt_a, wvt_b], axis=0)
    na, nbt = wvt_a.shape[0], wvt_b.shape[0]
    ones_rows = (tuple(range(A_V_DIM, na, A_VT_ROWS))
                 + tuple(range(na + HEAD_DIM, na + nbt, B_VT_ROWS)))
    gain = norm_mix[0][None, :]

    pa, pb, vta, vtb, gates = _in_proj(
        x2d, gain, w_in_b, wvt, ((0, qk_w), (pb_lo, vb_lo)), gates_from,
        ((0, na, A_KEY_CHUNK), (na, na + nbt, IN_PROJ_ROW_TILE)), ones_rows, tm=IN_PROJ_ROW_TILE)
    pa_m, va_m, pb_m, vb_m = _in_proj(
        meta_tokens.astype(F32), gain, w_in_b, wvt,
        ((0, qk_w), (qk_w, pb_lo), (pb_lo, vb_lo), (vb_lo, gates_from)), None, (), (), tm=N_META)
    pad = ((0, META_PAD - N_META), (0, 0))
    pa_m = jnp.pad(pa_m, pad)
    pb_m = jnp.pad(pb_m, pad)
    vta_m = _extend_heads(jnp.pad(va_m, pad).T.reshape(a_heads, A_V_DIM, META_PAD), 1)
    vtb_m = _extend_heads(jnp.pad(vb_m, pad).T.reshape(b_kv, HEAD_DIM, META_PAD), 1)

    slopes = jnp.asarray([2.0 ** (-8.0 * (i + 1) / a_heads) for i in range(a_heads)], F32)
    ya = _attn_a(pa, vta, pa_m, vta_m, slopes, lambda_q1, lambda_k1, lambda_q2, lambda_k2,
                 subln_gain[0][:, None], batch, seq, a_heads,
                 tq=A_QUERY_TILE, tk=A_KEY_CHUNK)
    yb = _attn_b(pb, vtb, pb_m, vtb_m, sink_logits[0], batch, seq, b_heads, b_kv,
                 nblk=B_BLOCKS_PER_STEP, slab=IN_PROJ_ROW_TILE)

    out = _post(x2d, ya, yb, gates,
                w_branch_a[0].astype(BF16), w_branch_b[0].astype(BF16), w_out[0].astype(BF16),
                norm_ffn[0][None, :], w_ff_gate[0].astype(BF16), w_ff_up[0].astype(BF16),
                w_ff_down[0].astype(BF16), norm_final[None, :], tm=POST_ROW_TILE)
    return out.reshape(batch, seq, d)
```

```python
import functools
import math

import ml_dtypes
import numpy as np

import jax
import jax.numpy as jnp
from jax import lax
from jax.experimental import pallas as pl
from jax.experimental.pallas import tpu as pltpu

F32 = jnp.float32
BF16 = jnp.bfloat16

N_META = 16
HEAD_DIM = 64
WINDOW = 128
BLOCK = 128
A_V_DIM = 2 * HEAD_DIM
EPS = 1e-6
LAMBDA_INIT = 0.8 - 0.6 * math.exp(-0.3 * 0)
SCALE = HEAD_DIM ** -0.5
LOG2E = math.log2(math.e)

LANES = 128
BF16_SUBLANES = 16
META_PAD = 128
NEG = -1e30
VMEM_LIMIT_BYTES = 56 * 1024 * 1024

IN_PROJ_ROW_TILE = 1024
A_KEY_CHUNK = 512
A_QUERY_TILE = 1024
A_SCORES_AHEAD = 1
A_HEADS_PER_STEP = 2
POST_ROW_TILE = 512
POST_ROW_GROUPS = 2
FFN_COLUMN_GROUP = 1536

A_VT_ROWS = A_V_DIM + BF16_SUBLANES
B_VT_ROWS = HEAD_DIM + BF16_SUBLANES
B_BLOCKS_PER_STEP = 8
B_SCORES_AHEAD = 2
_BLOCK_BITS = BLOCK.bit_length() - 1

_DN_NT = (((1,), (1,)), ((), ()))


def _resident(block_shape, index_map):
    return pl.BlockSpec(block_shape, index_map, pipeline_mode=pl.Buffered(1))


def _in_proj_kernel(x_ref, g_ref, w_ref, wvt_ref, *out_refs, splits, gates_from, vt_splits,
                    ones_rows):
    x = x_ref[...]
    ms = jnp.mean(x * x, axis=-1, keepdims=True)
    hn = ((x * lax.rsqrt(ms + EPS)) * g_ref[...]).astype(BF16)
    if gates_from is not None:
        g = jnp.dot(hn, w_ref[:, gates_from:], preferred_element_type=F32)
        out_refs[-1][...] = jax.nn.sigmoid(g).astype(BF16)
    for (lo, hi), o_ref in zip(splits, out_refs):
        o_ref[...] = jnp.dot(hn, w_ref[:, lo:hi], preferred_element_type=F32).astype(BF16)
    if gates_from is not None:
        vt_refs = out_refs[len(splits):-1]
        vt = lax.dot_general(wvt_ref[...], hn, _DN_NT, preferred_element_type=F32)
        row = lax.broadcasted_iota(jnp.int32, vt.shape, 0)
        is_ones = functools.reduce(jnp.logical_or, [row == r for r in ones_rows])
        vt = jnp.where(is_ones, 1.0, vt).astype(BF16)
        for (lo, hi, slab), vt_ref in zip(vt_splits, vt_refs):
            for j in range(vt.shape[1] // slab):
                vt_ref[j] = vt[lo:hi, j * slab:(j + 1) * slab]


def _in_proj(x2d, gain, w_bf16, wvt_bf16, splits, gates_from, vt_splits, ones_rows, tm):
    m, d = x2d.shape
    n_all = w_bf16.shape[1]
    nv = wvt_bf16.shape[0]
    widths = [hi - lo for lo, hi in splits]
    out_shape = [jax.ShapeDtypeStruct((m, w), BF16) for w in widths]
    out_specs = [pl.BlockSpec((tm, w), lambda i: (i, 0)) for w in widths]
    if gates_from is not None:
        for lo, hi, slab in vt_splits:
            out_shape.append(jax.ShapeDtypeStruct((m // slab, hi - lo, slab), BF16))
            out_specs.append(pl.BlockSpec((tm // slab, hi - lo, slab), lambda i: (i, 0, 0)))
        out_shape.append(jax.ShapeDtypeStruct((m, n_all - gates_from), BF16))
        out_specs.append(pl.BlockSpec((tm, n_all - gates_from), lambda i: (i, 0)))
    return pl.pallas_call(
        functools.partial(_in_proj_kernel, splits=splits, gates_from=gates_from,
                          vt_splits=vt_splits, ones_rows=ones_rows),
        grid=(m // tm,),
        in_specs=[
            pl.BlockSpec((tm, d), lambda i: (i, 0)),
            _resident((1, d), lambda i: (0, 0)),
            _resident((d, n_all), lambda i: (0, 0)),
            _resident((nv, d), lambda i: (0, 0)),
        ],
        out_specs=out_specs,
        out_shape=out_shape,
        compiler_params=pltpu.CompilerParams(
            dimension_semantics=("arbitrary",), vmem_limit_bytes=VMEM_LIMIT_BYTES),
        name="in_proj_meta" if gates_from is None else "in_proj_gates",
    )(x2d, gain, w_bf16, wvt_bf16)


_FEAT_SPLIT_BITS = 6
_N_PIECES = 3


def _bf16_pieces(x, n):
    pieces, rest = [], np.float32(x)
    for _ in range(n):
        piece = np.float32(np.asarray(rest, dtype=ml_dtypes.bfloat16))
        pieces.append(piece)
        rest = np.float32(rest - piece)
    return pieces


def _feature_table():
    e = _bf16_pieces(LOG2E, _N_PIECES)
    t = np.zeros((8, LANES), np.float32)
    n = _N_PIECES
    t[0, 0:n] = 1.0
    t[1, n:2 * n] = 1.0
    t[2, 2 * n:3 * n] = e
    t[2, 3 * n:4 * n] = e
    t[3, 2 * n:3 * n] = 1.0
    t[4, 3 * n:4 * n] = 1.0
    t[5, 0:n] = e
    t[5, n:2 * n] = e
    return jnp.asarray(t)


def _pos_terms(pos, slope):
    hi = (pos >> _FEAT_SPLIT_BITS).astype(F32) * (slope * (1 << _FEAT_SPLIT_BITS))
    lo = (pos & ((1 << _FEAT_SPLIT_BITS) - 1)).astype(F32) * slope
    return hi, lo


def _attn_a_kernel(slopes_ref, lq1_ref, lk1_ref, lq2_ref, lk2_ref, gain_ref, feat_ref,
                   q_ref, k_ref, vt_ref, km_ref, vtm_ref, o_ref,
                   kaug_ref, qfeat_ref, negabs_ref, *, tq, tk, seq, hps):
    b = pl.program_id(0)
    hp = pl.program_id(1)
    qi = pl.program_id(2)
    nqc = 2 * tq
    n_chunks = seq // tk
    n_off = max(1, tk // tq)
    n_diag = max(1, tq // tk)

    @pl.when((b == 0) & (hp == 0) & (qi == 0))
    def _build_distance_tiles():
        for v in range(max(n_off, n_diag)):
            start = v * tq if tk >= tq else -v * tk
            d = ((lax.broadcasted_iota(jnp.int32, (tk, nqc), 1) & (tq - 1))
                 - lax.broadcasted_iota(jnp.int32, (tk, nqc), 0) + start)
            negabs_ref[v] = -jnp.abs(d).astype(F32)

    @pl.when(qi == 0)
    def _build_features():
        def fill(c, carry):
            rows = pl.ds(pl.multiple_of(c * tk, tk), tk)
            pos = c * tk + lax.broadcasted_iota(jnp.int32, (tk, LANES), 0)
            for j in range(hps):
                hi, lo = _pos_terms(pos, slopes_ref[hp * hps + j])
                kaug_ref[j, rows, :LANES] = k_ref[rows, j * LANES:(j + 1) * LANES]
                kf = hi * feat_ref[0:1, :] + lo * feat_ref[1:2, :] + feat_ref[2:3, :]
                kaug_ref[j, rows, LANES:] = kf.astype(BF16)
                qf = feat_ref[5:6, :] - hi * feat_ref[3:4, :] - lo * feat_ref[4:5, :]
                qfeat_ref[j, rows, :] = qf.astype(BF16)
            return carry
        lax.fori_loop(0, n_chunks, fill, 0)

    lam = (jnp.exp(jnp.sum(lq1_ref[...] * lk1_ref[...], axis=-1, keepdims=True))
           - jnp.exp(jnp.sum(lq2_ref[...] * lk2_ref[...], axis=-1, keepdims=True))
           + LAMBDA_INIT)
    lane = lax.broadcasted_iota(jnp.int32, (tq, LANES), 1)
    meta_valid = lax.broadcasted_iota(jnp.int32, (META_PAD, nqc), 0) < N_META
    c_diag = (qi * tq) // tk

    def scores(k_rows, q_aug_t):
        return jnp.dot(k_rows, q_aug_t, preferred_element_type=F32)

    def softmax(s, m):
        m_new = jnp.maximum(m, jnp.max(s, axis=0, keepdims=True))
        return m_new, jnp.exp2(s - m_new).astype(BF16), jnp.exp2(m - m_new)

    def chunk_of(t):
        if t < n_diag:
            return c_diag + t
        u = t - n_diag
        return u + (u >= c_diag).astype(jnp.int32) * n_diag

    def one_head(j):
        slope = slopes_ref[hp * hps + j]
        vt_rows = slice(j * A_VT_ROWS, (j + 1) * A_VT_ROWS)
        q = q_ref[:, j * LANES:(j + 1) * LANES]
        zero = jnp.zeros_like(q)
        qq = jnp.concatenate([jnp.where(lane < HEAD_DIM, q, zero),
                              jnp.where(lane >= HEAD_DIM, q, zero)], axis=0)
        qf = qfeat_ref[j, pl.ds(pl.multiple_of(qi * tq, tq), tq), :]
        qf = jnp.concatenate([qf, qf], axis=0)
        qq_t, qf_t = qq.T, qf.T
        q_left = jnp.concatenate([qq_t, qf_t], axis=0)
        q_right = jnp.concatenate([qq_t, -qf_t], axis=0)
        q_plain = jnp.concatenate([qq_t, jnp.zeros_like(qf_t)], axis=0)

        def chunk_scores(t):
            c = chunk_of(t)
            k_rows = kaug_ref[j, pl.ds(pl.multiple_of(c * tk, tk), tk), :]
            if t < n_diag:
                variant = qi - c_diag * n_off if tk >= tq else t
                return scores(k_rows, q_plain) + negabs_ref[variant] * (slope * LOG2E)
            return scores(k_rows, jnp.where(c < c_diag, q_left, q_right))

        km_aug = jnp.concatenate([km_ref[:, j * LANES:(j + 1) * LANES],
                                  jnp.zeros((META_PAD, LANES), BF16)], axis=1)
        s = jnp.where(meta_valid, scores(km_aug, q_plain), NEG)
        m = jnp.max(s, axis=0, keepdims=True)
        acc = jnp.dot(vtm_ref[vt_rows, :], jnp.exp2(s - m).astype(BF16),
                      preferred_element_type=F32)

        pending = [chunk_scores(t) for t in range(min(A_SCORES_AHEAD, n_chunks))]
        for t in range(n_chunks):
            if t + A_SCORES_AHEAD < n_chunks:
                pending.append(chunk_scores(t + A_SCORES_AHEAD))
            m, p, alpha = softmax(pending.pop(0), m)
            acc = alpha * acc + jnp.dot(vt_ref[chunk_of(t), vt_rows, :], p,
                                        preferred_element_type=F32)

        o = acc[:A_V_DIM] / acc[A_V_DIM:A_V_DIM + 1]
        y = o[:, :tq] - lam * o[:, tq:]
        ms = jnp.mean(y * y, axis=0, keepdims=True)
        y = ((y * lax.rsqrt(ms + EPS)) * gain_ref[...]) * (1.0 - LAMBDA_INIT)
        o_ref[:, j * A_V_DIM:(j + 1) * A_V_DIM] = y.T.astype(BF16)

    for j in range(hps):
        one_head(j)


def _attn_a(pa, vt, pa_meta, vt_meta, slopes, lq1, lk1, lq2, lk2, gain_col,
            batch, seq, heads, tq, tk, hps):
    nq = seq // tq
    width = heads * A_V_DIM
    assert max(tk, tq) % min(tk, tq) == 0 and seq % tk == 0 and seq % tq == 0
    assert heads % hps == 0
    kcol = heads // hps
    vec = lambda n: _resident((1, n), lambda b, h, i: (0, 0))
    return pl.pallas_call(
        functools.partial(_attn_a_kernel, tq=tq, tk=tk, seq=seq, hps=hps),
        grid=(batch, heads // hps, nq),
        in_specs=[
            pl.BlockSpec(memory_space=pltpu.SMEM),
            vec(HEAD_DIM), vec(HEAD_DIM), vec(HEAD_DIM), vec(HEAD_DIM),
            _resident((A_V_DIM, 1), lambda b, h, i: (0, 0)),
            _resident((8, LANES), lambda b, h, i: (0, 0)),
            pl.BlockSpec((tq, hps * LANES), lambda b, h, i: (b * nq + i, h)),
            pl.BlockSpec((seq, hps * LANES), lambda b, h, i: (b, kcol + h)),
            pl.BlockSpec((seq // tk, hps * A_VT_ROWS, tk), lambda b, h, i: (b, h, 0)),
            pl.BlockSpec((META_PAD, hps * LANES), lambda b, h, i: (0, kcol + h)),
            pl.BlockSpec((hps * A_VT_ROWS, META_PAD), lambda b, h, i: (h, 0)),
        ],
        out_specs=pl.BlockSpec((tq, hps * A_V_DIM), lambda b, h, i: (b * nq + i, h)),
        out_shape=jax.ShapeDtypeStruct((batch * seq, width), BF16),
        scratch_shapes=[pltpu.VMEM((hps, seq, 2 * LANES), BF16),
                        pltpu.VMEM((hps, seq, LANES), BF16),
                        pltpu.VMEM((max(tk // tq, tq // tk), tk, 2 * tq), F32)],
        compiler_params=pltpu.CompilerParams(
            dimension_semantics=("arbitrary", "arbitrary", "arbitrary"),
            vmem_limit_bytes=VMEM_LIMIT_BYTES),
        name="attn_a",
    )(slopes, lq1, lk1, lq2, lk2, gain_col, _feature_table(), pa, pa, vt, pa_meta, vt_meta)


def _attn_b_kernel(sink_ref, q_ref, kprev_ref, kcur_ref, knext_ref, vprev_ref, vcur_ref,
                   vnext_ref, km_ref, vtm_ref, o_ref, bias_ref, *, seq, heads, group, nblk):
    b = pl.program_id(0)
    step = pl.program_id(1)
    nb = seq // BLOCK
    kv_heads = heads // group
    nwin = 3 * BLOCK
    nkeys = nwin + N_META
    ncols = heads * BLOCK
    gcols = group * BLOCK
    k_pad = -(-nkeys // (2 * LANES)) * (2 * LANES)

    @pl.when((b == 0) & (step == 0))
    def _build_bias():
        kr = lax.broadcasted_iota(jnp.int32, (nkeys, ncols), 0)
        col = lax.broadcasted_iota(jnp.int32, (nkeys, ncols), 1)
        dt = jnp.abs((col & (BLOCK - 1)) - (kr - BLOCK))
        head = (col >> _BLOCK_BITS).astype(F32)
        slope = jnp.exp2(-(head + 1.0) * (8.0 / heads))
        base = jnp.where((kr < nwin) & (dt <= WINDOW), dt.astype(F32) * (-LOG2E) * slope,
                         jnp.where(kr >= nwin, 0.0, NEG))
        bias_ref[0] = base
        bias_ref[1] = jnp.where(kr < BLOCK, NEG, base)
        bias_ref[2] = jnp.where((kr >= 2 * BLOCK) & (kr < nwin), NEG, base)

    col_head = lax.broadcasted_iota(jnp.int32, (1, ncols), 1) >> _BLOCK_BITS
    sink_row = jnp.zeros((1, ncols), F32)
    for hd in range(heads):
        sink_row = jnp.where(col_head == hd, sink_ref[hd] * LOG2E, sink_row)

    lane = lax.broadcasted_iota(jnp.int32, (BLOCK, LANES), 1)
    lo = lane < HEAD_DIM
    zero = jnp.zeros((BLOCK, LANES), BF16)
    k_blocks = ([kprev_ref[...]]
                + [kcur_ref[j * BLOCK:(j + 1) * BLOCK, :] for j in range(nblk)] + [knext_ref[...]])
    vt_blocks = ([vprev_ref[0]]
                 + [vcur_ref[0, :, j * BLOCK:(j + 1) * BLOCK] for j in range(nblk)] + [vnext_ref[0]])
    k_meta = km_ref[:N_META, :]
    vt_meta = vtm_ref[...]

    def block_scores(blk):
        i = step * nblk + blk
        k_win = jnp.concatenate(k_blocks[blk:blk + 3] + [k_meta], axis=0)

        q = q_ref[blk * BLOCK:(blk + 1) * BLOCK, :]
        q_heads = []
        for hd in range(heads):
            part = q[:, (hd // 2) * LANES:(hd // 2 + 1) * LANES]
            in_lo, want_lo = hd % 2 == 0, (hd // group) % 2 == 0
            if in_lo != want_lo:
                part = jnp.concatenate([part[:, HEAD_DIM:], part[:, :HEAD_DIM]], axis=1)
            q_heads.append(jnp.where(lo if want_lo else ~lo, part, zero))
        q_all = jnp.concatenate(q_heads, axis=0)

        variant = jnp.where(i == 0, 1, jnp.where(i == nb - 1, 2, 0))
        return (lax.dot_general(k_win, q_all, _DN_NT, preferred_element_type=F32)
                + bias_ref[variant])

    def block_output(blk, s):
        vt_win = jnp.concatenate(vt_blocks[blk:blk + 3] + [vt_meta], axis=1)
        m = jnp.maximum(jnp.max(s, axis=0, keepdims=True), sink_row)
        p = jnp.exp2(s - m).astype(BF16)
        p = jnp.concatenate([p, jnp.zeros((k_pad - nkeys, ncols), BF16)], axis=0)
        sink_term = jnp.exp2(sink_row - m)

        outs = []
        for g in range(kv_heads):
            cols = slice(g * gcols, (g + 1) * gcols)
            og = jnp.dot(vt_win[g * B_VT_ROWS:(g + 1) * B_VT_ROWS], p[:, cols],
                         preferred_element_type=F32)
            outs.append(og[:HEAD_DIM] / (og[HEAD_DIM:HEAD_DIM + 1] + sink_term[:, cols]))
        pairs = []
        for j in range(heads // 2):
            g, first = (2 * j) // group, (2 * j) % group
            pair = outs[g][:, first * BLOCK:(first + 2) * BLOCK]
            pairs.append(jnp.concatenate([pair[:, :BLOCK], pair[:, BLOCK:]], axis=0).T)
        o_ref[blk * BLOCK:(blk + 1) * BLOCK, :] = jnp.concatenate(pairs, axis=1).astype(BF16)

    pending = [block_scores(blk) for blk in range(min(B_SCORES_AHEAD, nblk))]
    for blk in range(nblk):
        if blk + B_SCORES_AHEAD < nblk:
            pending.append(block_scores(blk + B_SCORES_AHEAD))
        block_output(blk, pending.pop(0))


def _attn_b(pb, vtb, pb_meta, vtb_meta, sink, batch, seq, heads, kv_heads, nblk, slab):
    nb = seq // BLOCK
    n_steps = nb // nblk
    qw = heads * HEAD_DIM
    kcol = qw // LANES
    rows = kv_heads * B_VT_ROWS
    per_slab = slab // BLOCK
    assert kv_heads * HEAD_DIM == LANES and nb % nblk == 0 and per_slab % nblk == 0
    prev_blk = lambda i: jnp.maximum(nblk * i - 1, 0)
    next_blk = lambda i: jnp.minimum(nblk * i + nblk, nb - 1)
    vt_spec = lambda blk_of: pl.BlockSpec(
        (1, rows, BLOCK),
        lambda b, i: (b * (nb // per_slab) + blk_of(i) // per_slab, 0, blk_of(i) % per_slab))
    return pl.pallas_call(
        functools.partial(_attn_b_kernel, seq=seq, heads=heads, group=heads // kv_heads, nblk=nblk),
        grid=(batch, n_steps),
        in_specs=[
            pl.BlockSpec(memory_space=pltpu.SMEM),
            pl.BlockSpec((nblk * BLOCK, qw), lambda b, i: (b * n_steps + i, 0)),
            pl.BlockSpec((BLOCK, LANES), lambda b, i: (b * nb + prev_blk(i), kcol)),
            pl.BlockSpec((nblk * BLOCK, LANES), lambda b, i: (b * n_steps + i, kcol)),
            pl.BlockSpec((BLOCK, LANES), lambda b, i: (b * nb + next_blk(i), kcol)),
            vt_spec(prev_blk),
            pl.BlockSpec((1, rows, nblk * BLOCK),
                         lambda b, i: (b * (nb // per_slab) + (nblk * i) // per_slab, 0,
                                       i % (per_slab // nblk))),
            vt_spec(next_blk),
            _resident((META_PAD, LANES), lambda b, i: (0, kcol)),
            _resident((rows, META_PAD), lambda b, i: (0, 0)),
        ],
        out_specs=pl.BlockSpec((nblk * BLOCK, qw), lambda b, i: (b * n_steps + i, 0)),
        out_shape=jax.ShapeDtypeStruct((batch * seq, qw), BF16),
        scratch_shapes=[pltpu.VMEM((3, 3 * BLOCK + N_META, heads * BLOCK), F32)],
        compiler_params=pltpu.CompilerParams(
            dimension_semantics=("arbitrary", "arbitrary"),
            vmem_limit_bytes=VMEM_LIMIT_BYTES),
        name="attn_b",
    )(sink, pb, pb, pb, pb, vtb, vtb, vtb, pb_meta, vtb_meta)


def _rms(x, g):
    ms = jnp.mean(x * x, axis=-1, keepdims=True)
    return (x * lax.rsqrt(ms + EPS)) * g


def _post_kernel(x_ref, ya_ref, yb_ref, gate_ref, wba_ref, wbb_ref, wo_ref, nf_ref,
                 wg_ref, wu_ref, wd_ref, nfin_ref, o_ref, *, d_model):
    dot = functools.partial(jnp.dot, preferred_element_type=F32)
    rows = x_ref.shape[0] // POST_ROW_GROUPS
    groups = [slice(r * rows, (r + 1) * rows) for r in range(POST_ROW_GROUPS)]
    h1 = []
    for r in groups:
        ga = gate_ref[r, :d_model].astype(F32)
        gb = gate_ref[r, d_model:].astype(F32)
        merged = ga * dot(ya_ref[r, :], wba_ref[...]) + gb * dot(yb_ref[r, :], wbb_ref[...])
        h1.append(x_ref[r, :] + dot(merged.astype(BF16), wo_ref[...]))
    hn = [_rms(h, nf_ref[...]).astype(BF16) for h in h1]
    d_ff = wg_ref.shape[1]
    bounds = [min(k * FFN_COLUMN_GROUP, d_ff) for k in range(-(-d_ff // FFN_COLUMN_GROUP) + 1)]
    h2 = list(h1)
    for lo, hi in zip(bounds[:-1], bounds[1:]):
        for i in range(POST_ROW_GROUPS):
            act = jax.nn.silu(dot(hn[i], wg_ref[:, lo:hi])) * dot(hn[i], wu_ref[:, lo:hi])
            h2[i] = h2[i] + dot(act.astype(BF16), wd_ref[lo:hi, :])
    for r, h in zip(groups, h2):
        o_ref[r, :] = _rms(h, nfin_ref[...])


def _post(x2d, ya, yb, gates, wba, wbb, wo, nf, wg, wu, wd, nfin, tm):
    m, d = x2d.shape
    row = lambda w: pl.BlockSpec((tm, w), lambda i: (i, 0))
    full = lambda a: _resident(a.shape, lambda i: (0, 0))
    return pl.pallas_call(
        functools.partial(_post_kernel, d_model=d),
        grid=(m // tm,),
        in_specs=[row(d), row(ya.shape[1]), row(yb.shape[1]), row(2 * d),
                  full(wba), full(wbb), full(wo), full(nf),
                  full(wg), full(wu), full(wd), full(nfin)],
        out_specs=row(d),
        out_shape=jax.ShapeDtypeStruct((m, d), F32),
        compiler_params=pltpu.CompilerParams(
            dimension_semantics=("arbitrary",), vmem_limit_bytes=VMEM_LIMIT_BYTES),
        name="post_ffn",
    )(x2d, ya, yb, gates, wba, wbb, wo, nf, wg, wu, wd, nfin)


def _extend_heads(vt_heads, first_row):
    heads, dim, n = vt_heads.shape
    extra = jnp.zeros((heads, BF16_SUBLANES, n), vt_heads.dtype).at[:, 0, :].set(first_row)
    return jnp.concatenate([vt_heads, extra], axis=1).reshape(heads * (dim + BF16_SUBLANES), n)


def kernel(x, meta_tokens, norm_mix, w_in, lambda_q1, lambda_k1, lambda_q2, lambda_k2,
           subln_gain, sink_logits, w_branch_a, w_branch_b, w_out, norm_ffn,
           w_ff_gate, w_ff_up, w_ff_down, norm_final):
    batch, seq, d = x.shape
    assert norm_mix.shape[0] == 1, "single-layer block"
    a_heads = w_branch_a.shape[1] // A_V_DIM
    assert 8 % a_heads == 0, "mixer A's bf16 position features need power-of-two ALiBi slopes"
    assert (seq - 1) >> _FEAT_SPLIT_BITS < 256, "position feature pieces must fit 8 bits"
    b_heads = sink_logits.shape[1]
    b_kv = max(1, b_heads // 4)
    qa_w = a_heads * 2 * HEAD_DIM
    qk_w = 2 * qa_w
    va_w = a_heads * A_V_DIM
    qb_w = b_heads * HEAD_DIM
    kvb_w = b_kv * HEAD_DIM
    pb_lo = qk_w + va_w
    vb_lo = pb_lo + qb_w + kvb_w
    gates_from = vb_lo + kvb_w

    x2d = x.reshape(batch * seq, d)
    col_scale = (jnp.ones((w_in.shape[2],), F32)
                 .at[:qa_w].set(SCALE * LOG2E).at[pb_lo:pb_lo + qb_w].set(SCALE * LOG2E))
    w_in_b = (w_in[0] * col_scale).astype(BF16)
    wvt_a = _extend_heads(w_in_b[:, qk_w:pb_lo].T.reshape(a_heads, A_V_DIM, d), 0)
    wvt_b = _extend_heads(w_in_b[:, vb_lo:gates_from].T.reshape(b_kv, HEAD_DIM, d), 0)
    wvt = jnp.concatenate([wvt_a, wvt_b], axis=0)
    na, nbt = wvt_a.shape[0], wvt_b.shape[0]
    ones_rows = (tuple(range(A_V_DIM, na, A_VT_ROWS))
                 + tuple(range(na + HEAD_DIM, na + nbt, B_VT_ROWS)))
    gain = norm_mix[0][None, :]

    pa, pb, vta, vtb, gates = _in_proj(
        x2d, gain, w_in_b, wvt, ((0, qk_w), (pb_lo, vb_lo)), gates_from,
        ((0, na, A_KEY_CHUNK), (na, na + nbt, IN_PROJ_ROW_TILE)), ones_rows, tm=IN_PROJ_ROW_TILE)
    pa_m, va_m, pb_m, vb_m = _in_proj(
        meta_tokens.astype(F32), gain, w_in_b, wvt,
        ((0, qk_w), (qk_w, pb_lo), (pb_lo, vb_lo), (vb_lo, gates_from)), None, (), (), tm=N_META)
    pad = ((0, META_PAD - N_META), (0, 0))
    pa_m = jnp.pad(pa_m, pad)
    pb_m = jnp.pad(pb_m, pad)
    vta_m = _extend_heads(jnp.pad(va_m, pad).T.reshape(a_heads, A_V_DIM, META_PAD), 1)
    vtb_m = _extend_heads(jnp.pad(vb_m, pad).T.reshape(b_kv, HEAD_DIM, META_PAD), 1)

    slopes = jnp.asarray([2.0 ** (-8.0 * (i + 1) / a_heads) for i in range(a_heads)], F32)
    ya = _attn_a(pa, vta, pa_m, vta_m, slopes, lambda_q1, lambda_k1, lambda_q2, lambda_k2,
                 subln_gain[0][:, None], batch, seq, a_heads,
                 tq=A_QUERY_TILE, tk=A_KEY_CHUNK, hps=A_HEADS_PER_STEP)
    yb = _attn_b(pb, vtb, pb_m, vtb_m, sink_logits[0], batch, seq, b_heads, b_kv,
                 nblk=B_BLOCKS_PER_STEP, slab=IN_PROJ_ROW_TILE)

    out = _post(x2d, ya, yb, gates,
                w_branch_a[0].astype(BF16), w_branch_b[0].astype(BF16), w_out[0].astype(BF16),
                norm_ffn[0][None, :], w_ff_gate[0].astype(BF16), w_ff_up[0].astype(BF16),
                w_ff_down[0].astype(BF16), norm_final[None, :], tm=POST_ROW_TILE)
    return out.reshape(batch, seq, d)
```

```python
import functools
import math

import ml_dtypes
import numpy as np

import jax
import jax.numpy as jnp
from jax import lax
from jax.experimental import pallas as pl
from jax.experimental.pallas import tpu as pltpu

F32 = jnp.float32
BF16 = jnp.bfloat16

N_META = 16
HEAD_DIM = 64
WINDOW = 128
BLOCK = 128
A_V_DIM = 2 * HEAD_DIM
EPS = 1e-6
LAMBDA_INIT = 0.8 - 0.6 * math.exp(-0.3 * 0)
SCALE = HEAD_DIM ** -0.5
LOG2E = math.log2(math.e)

LANES = 128
BF16_SUBLANES = 16
META_PAD = 128
NEG = -1e30
VMEM_LIMIT_BYTES = 56 * 1024 * 1024
POST_VMEM_LIMIT_BYTES = 62 * 1024 * 1024

IN_PROJ_ROW_TILE = 1024
A_KEY_CHUNK = 512
A_QUERY_TILE = 1024
A_SCORES_AHEAD = 1
A_HEADS_PER_STEP = 2
POST_ROW_TILE = 1024
POST_ROW_GROUPS = 4
FFN_COLUMN_GROUP = 1536

A_VT_ROWS = A_V_DIM + BF16_SUBLANES
B_VT_ROWS = HEAD_DIM + BF16_SUBLANES
B_BLOCKS_PER_STEP = 8
B_SCORES_AHEAD = 2
_BLOCK_BITS = BLOCK.bit_length() - 1

_DN_NT = (((1,), (1,)), ((), ()))


def _resident(block_shape, index_map):
    return pl.BlockSpec(block_shape, index_map, pipeline_mode=pl.Buffered(1))


def _in_proj_kernel(x_ref, g_ref, w_ref, wvt_ref, *out_refs, splits, gates_from, vt_splits,
                    ones_rows):
    x = x_ref[...]
    ms = jnp.mean(x * x, axis=-1, keepdims=True)
    hn = ((x * lax.rsqrt(ms + EPS)) * g_ref[...]).astype(BF16)
    if gates_from is not None:
        g = jnp.dot(hn, w_ref[:, gates_from:], preferred_element_type=F32)
        out_refs[-1][...] = jax.nn.sigmoid(g).astype(BF16)
    for (lo, hi), o_ref in zip(splits, out_refs):
        o_ref[...] = jnp.dot(hn, w_ref[:, lo:hi], preferred_element_type=F32).astype(BF16)
    if gates_from is not None:
        vt_refs = out_refs[len(splits):-1]
        vt = lax.dot_general(wvt_ref[...], hn, _DN_NT, preferred_element_type=F32)
        row = lax.broadcasted_iota(jnp.int32, vt.shape, 0)
        is_ones = functools.reduce(jnp.logical_or, [row == r for r in ones_rows])
        vt = jnp.where(is_ones, 1.0, vt).astype(BF16)
        for (lo, hi, slab), vt_ref in zip(vt_splits, vt_refs):
            for j in range(vt.shape[1] // slab):
                vt_ref[j] = vt[lo:hi, j * slab:(j + 1) * slab]


def _in_proj(x2d, gain, w_bf16, wvt_bf16, splits, gates_from, vt_splits, ones_rows, tm):
    m, d = x2d.shape
    n_all = w_bf16.shape[1]
    nv = wvt_bf16.shape[0]
    widths = [hi - lo for lo, hi in splits]
    out_shape = [jax.ShapeDtypeStruct((m, w), BF16) for w in widths]
    out_specs = [pl.BlockSpec((tm, w), lambda i: (i, 0)) for w in widths]
    if gates_from is not None:
        for lo, hi, slab in vt_splits:
            out_shape.append(jax.ShapeDtypeStruct((m // slab, hi - lo, slab), BF16))
            out_specs.append(pl.BlockSpec((tm // slab, hi - lo, slab), lambda i: (i, 0, 0)))
        out_shape.append(jax.ShapeDtypeStruct((m, n_all - gates_from), BF16))
        out_specs.append(pl.BlockSpec((tm, n_all - gates_from), lambda i: (i, 0)))
    return pl.pallas_call(
        functools.partial(_in_proj_kernel, splits=splits, gates_from=gates_from,
                          vt_splits=vt_splits, ones_rows=ones_rows),
        grid=(m // tm,),
        in_specs=[
            pl.BlockSpec((tm, d), lambda i: (i, 0)),
            _resident((1, d), lambda i: (0, 0)),
            _resident((d, n_all), lambda i: (0, 0)),
            _resident((nv, d), lambda i: (0, 0)),
        ],
        out_specs=out_specs,
        out_shape=out_shape,
        compiler_params=pltpu.CompilerParams(
            dimension_semantics=("arbitrary",), vmem_limit_bytes=VMEM_LIMIT_BYTES),
        name="in_proj_meta" if gates_from is None else "in_proj_gates",
    )(x2d, gain, w_bf16, wvt_bf16)


_FEAT_SPLIT_BITS = 6
_N_PIECES = 3


def _bf16_pieces(x, n):
    pieces, rest = [], np.float32(x)
    for _ in range(n):
        piece = np.float32(np.asarray(rest, dtype=ml_dtypes.bfloat16))
        pieces.append(piece)
        rest = np.float32(rest - piece)
    return pieces


def _feature_table():
    e = _bf16_pieces(LOG2E, _N_PIECES)
    t = np.zeros((8, LANES), np.float32)
    n = _N_PIECES
    t[0, 0:n] = 1.0
    t[1, n:2 * n] = 1.0
    t[2, 2 * n:3 * n] = e
    t[2, 3 * n:4 * n] = e
    t[3, 2 * n:3 * n] = 1.0
    t[4, 3 * n:4 * n] = 1.0
    t[5, 0:n] = e
    t[5, n:2 * n] = e
    return jnp.asarray(t)


def _pos_terms(pos, slope):
    hi = (pos >> _FEAT_SPLIT_BITS).astype(F32) * (slope * (1 << _FEAT_SPLIT_BITS))
    lo = (pos & ((1 << _FEAT_SPLIT_BITS) - 1)).astype(F32) * slope
    return hi, lo


def _attn_a_kernel(slopes_ref, lq1_ref, lk1_ref, lq2_ref, lk2_ref, gain_ref, feat_ref,
                   q_ref, k_ref, vt_ref, km_ref, vtm_ref, o_ref,
                   kaug_ref, qfeat_ref, negabs_ref, *, tq, tk, seq, hps):
    b = pl.program_id(0)
    hp = pl.program_id(1)
    qi = pl.program_id(2)
    nqc = 2 * tq
    n_chunks = seq // tk
    n_off = max(1, tk // tq)
    n_diag = max(1, tq // tk)

    @pl.when((b == 0) & (hp == 0) & (qi == 0))
    def _build_distance_tiles():
        for v in range(max(n_off, n_diag)):
            start = v * tq if tk >= tq else -v * tk
            d = ((lax.broadcasted_iota(jnp.int32, (tk, nqc), 1) & (tq - 1))
                 - lax.broadcasted_iota(jnp.int32, (tk, nqc), 0) + start)
            negabs_ref[v] = -jnp.abs(d).astype(F32)

    @pl.when(qi == 0)
    def _build_features():
        def fill(c, carry):
            rows = pl.ds(pl.multiple_of(c * tk, tk), tk)
            pos = c * tk + lax.broadcasted_iota(jnp.int32, (tk, LANES), 0)
            for j in range(hps):
                hi, lo = _pos_terms(pos, slopes_ref[hp * hps + j])
                kaug_ref[j, rows, :LANES] = k_ref[rows, j * LANES:(j + 1) * LANES]
                kf = hi * feat_ref[0:1, :] + lo * feat_ref[1:2, :] + feat_ref[2:3, :]
                kaug_ref[j, rows, LANES:] = kf.astype(BF16)
                qf = feat_ref[5:6, :] - hi * feat_ref[3:4, :] - lo * feat_ref[4:5, :]
                qfeat_ref[j, rows, :] = qf.astype(BF16)
            return carry
        lax.fori_loop(0, n_chunks, fill, 0)

    lam = (jnp.exp(jnp.sum(lq1_ref[...] * lk1_ref[...], axis=-1, keepdims=True))
           - jnp.exp(jnp.sum(lq2_ref[...] * lk2_ref[...], axis=-1, keepdims=True))
           + LAMBDA_INIT)
    lane = lax.broadcasted_iota(jnp.int32, (tq, LANES), 1)
    meta_valid = lax.broadcasted_iota(jnp.int32, (META_PAD, nqc), 0) < N_META
    c_diag = (qi * tq) // tk

    def scores(k_rows, q_aug_t):
        return jnp.dot(k_rows, q_aug_t, preferred_element_type=F32)

    def softmax(s, m):
        m_new = jnp.maximum(m, jnp.max(s, axis=0, keepdims=True))
        return m_new, jnp.exp2(s - m_new).astype(BF16), jnp.exp2(m - m_new)

    def chunk_of(t):
        if t < n_diag:
            return c_diag + t
        u = t - n_diag
        return u + (u >= c_diag).astype(jnp.int32) * n_diag

    def one_head(j):
        slope = slopes_ref[hp * hps + j]
        vt_rows = slice(j * A_VT_ROWS, (j + 1) * A_VT_ROWS)
        q = q_ref[:, j * LANES:(j + 1) * LANES]
        zero = jnp.zeros_like(q)
        qq = jnp.concatenate([jnp.where(lane < HEAD_DIM, q, zero),
                              jnp.where(lane >= HEAD_DIM, q, zero)], axis=0)
        qf = qfeat_ref[j, pl.ds(pl.multiple_of(qi * tq, tq), tq), :]
        qf = jnp.concatenate([qf, qf], axis=0)
        qq_t, qf_t = qq.T, qf.T
        q_left = jnp.concatenate([qq_t, qf_t], axis=0)
        q_right = jnp.concatenate([qq_t, -qf_t], axis=0)
        q_plain = jnp.concatenate([qq_t, jnp.zeros_like(qf_t)], axis=0)

        def chunk_scores(t):
            c = chunk_of(t)
            k_rows = kaug_ref[j, pl.ds(pl.multiple_of(c * tk, tk), tk), :]
            if t < n_diag:
                variant = qi - c_diag * n_off if tk >= tq else t
                return scores(k_rows, q_plain) + negabs_ref[variant] * (slope * LOG2E)
            return scores(k_rows, jnp.where(c < c_diag, q_left, q_right))

        km_aug = jnp.concatenate([km_ref[:, j * LANES:(j + 1) * LANES],
                                  jnp.zeros((META_PAD, LANES), BF16)], axis=1)
        s = jnp.where(meta_valid, scores(km_aug, q_plain), NEG)
        m = jnp.max(s, axis=0, keepdims=True)
        acc = jnp.dot(vtm_ref[vt_rows, :], jnp.exp2(s - m).astype(BF16),
                      preferred_element_type=F32)

        pending = [chunk_scores(t) for t in range(min(A_SCORES_AHEAD, n_chunks))]
        for t in range(n_chunks):
            if t + A_SCORES_AHEAD < n_chunks:
                pending.append(chunk_scores(t + A_SCORES_AHEAD))
            m, p, alpha = softmax(pending.pop(0), m)
            acc = alpha * acc + jnp.dot(vt_ref[chunk_of(t), vt_rows, :], p,
                                        preferred_element_type=F32)

        o = acc[:A_V_DIM] / acc[A_V_DIM:A_V_DIM + 1]
        y = o[:, :tq] - lam * o[:, tq:]
        ms = jnp.mean(y * y, axis=0, keepdims=True)
        y = ((y * lax.rsqrt(ms + EPS)) * gain_ref[...]) * (1.0 - LAMBDA_INIT)
        o_ref[:, j * A_V_DIM:(j + 1) * A_V_DIM] = y.T.astype(BF16)

    for j in range(hps):
        one_head(j)


def _attn_a(pa, vt, pa_meta, vt_meta, slopes, lq1, lk1, lq2, lk2, gain_col,
            batch, seq, heads, tq, tk, hps):
    nq = seq // tq
    width = heads * A_V_DIM
    assert max(tk, tq) % min(tk, tq) == 0 and seq % tk == 0 and seq % tq == 0
    assert heads % hps == 0
    kcol = heads // hps
    vec = lambda n: _resident((1, n), lambda b, h, i: (0, 0))
    return pl.pallas_call(
        functools.partial(_attn_a_kernel, tq=tq, tk=tk, seq=seq, hps=hps),
        grid=(batch, heads // hps, nq),
        in_specs=[
            pl.BlockSpec(memory_space=pltpu.SMEM),
            vec(HEAD_DIM), vec(HEAD_DIM), vec(HEAD_DIM), vec(HEAD_DIM),
            _resident((A_V_DIM, 1), lambda b, h, i: (0, 0)),
            _resident((8, LANES), lambda b, h, i: (0, 0)),
            pl.BlockSpec((tq, hps * LANES), lambda b, h, i: (b * nq + i, h)),
            pl.BlockSpec((seq, hps * LANES), lambda b, h, i: (b, kcol + h)),
            pl.BlockSpec((seq // tk, hps * A_VT_ROWS, tk), lambda b, h, i: (b, h, 0)),
            pl.BlockSpec((META_PAD, hps * LANES), lambda b, h, i: (0, kcol + h)),
            pl.BlockSpec((hps * A_VT_ROWS, META_PAD), lambda b, h, i: (h, 0)),
        ],
        out_specs=pl.BlockSpec((tq, hps * A_V_DIM), lambda b, h, i: (b * nq + i, h)),
        out_shape=jax.ShapeDtypeStruct((batch * seq, width), BF16),
        scratch_shapes=[pltpu.VMEM((hps, seq, 2 * LANES), BF16),
                        pltpu.VMEM((hps, seq, LANES), BF16),
                        pltpu.VMEM((max(tk // tq, tq // tk), tk, 2 * tq), F32)],
        compiler_params=pltpu.CompilerParams(
            dimension_semantics=("arbitrary", "arbitrary", "arbitrary"),
            vmem_limit_bytes=VMEM_LIMIT_BYTES),
        name="attn_a",
    )(slopes, lq1, lk1, lq2, lk2, gain_col, _feature_table(), pa, pa, vt, pa_meta, vt_meta)


def _attn_b_kernel(sink_ref, q_ref, kprev_ref, kcur_ref, knext_ref, vprev_ref, vcur_ref,
                   vnext_ref, km_ref, vtm_ref, o_ref, bias_ref, *, seq, heads, group, nblk):
    b = pl.program_id(0)
    step = pl.program_id(1)
    nb = seq // BLOCK
    kv_heads = heads // group
    nwin = 3 * BLOCK
    nkeys = nwin + N_META
    ncols = heads * BLOCK
    gcols = group * BLOCK
    k_pad = -(-nkeys // (2 * LANES)) * (2 * LANES)

    @pl.when((b == 0) & (step == 0))
    def _build_bias():
        kr = lax.broadcasted_iota(jnp.int32, (nkeys, ncols), 0)
        col = lax.broadcasted_iota(jnp.int32, (nkeys, ncols), 1)
        dt = jnp.abs((col & (BLOCK - 1)) - (kr - BLOCK))
        head = (col >> _BLOCK_BITS).astype(F32)
        slope = jnp.exp2(-(head + 1.0) * (8.0 / heads))
        base = jnp.where((kr < nwin) & (dt <= WINDOW), dt.astype(F32) * (-LOG2E) * slope,
                         jnp.where(kr >= nwin, 0.0, NEG))
        bias_ref[0] = base
        bias_ref[1] = jnp.where(kr < BLOCK, NEG, base)
        bias_ref[2] = jnp.where((kr >= 2 * BLOCK) & (kr < nwin), NEG, base)

    col_head = lax.broadcasted_iota(jnp.int32, (1, ncols), 1) >> _BLOCK_BITS
    sink_row = jnp.zeros((1, ncols), F32)
    for hd in range(heads):
        sink_row = jnp.where(col_head == hd, sink_ref[hd] * LOG2E, sink_row)

    lane = lax.broadcasted_iota(jnp.int32, (BLOCK, LANES), 1)
    lo = lane < HEAD_DIM
    zero = jnp.zeros((BLOCK, LANES), BF16)
    k_blocks = ([kprev_ref[...]]
                + [kcur_ref[j * BLOCK:(j + 1) * BLOCK, :] for j in range(nblk)] + [knext_ref[...]])
    vt_blocks = ([vprev_ref[0]]
                 + [vcur_ref[0, :, j * BLOCK:(j + 1) * BLOCK] for j in range(nblk)] + [vnext_ref[0]])
    k_meta = km_ref[:N_META, :]
    vt_meta = vtm_ref[...]

    def block_scores(blk):
        i = step * nblk + blk
        k_win = jnp.concatenate(k_blocks[blk:blk + 3] + [k_meta], axis=0)

        q = q_ref[blk * BLOCK:(blk + 1) * BLOCK, :]
        q_heads = []
        for hd in range(heads):
            part = q[:, (hd // 2) * LANES:(hd // 2 + 1) * LANES]
            in_lo, want_lo = hd % 2 == 0, (hd // group) % 2 == 0
            if in_lo != want_lo:
                part = jnp.concatenate([part[:, HEAD_DIM:], part[:, :HEAD_DIM]], axis=1)
            q_heads.append(jnp.where(lo if want_lo else ~lo, part, zero))
        q_all = jnp.concatenate(q_heads, axis=0)

        variant = jnp.where(i == 0, 1, jnp.where(i == nb - 1, 2, 0))
        return (lax.dot_general(k_win, q_all, _DN_NT, preferred_element_type=F32)
                + bias_ref[variant])

    def block_output(blk, s):
        vt_win = jnp.concatenate(vt_blocks[blk:blk + 3] + [vt_meta], axis=1)
        m = jnp.maximum(jnp.max(s, axis=0, keepdims=True), sink_row)
        p = jnp.exp2(s - m).astype(BF16)
        p = jnp.concatenate([p, jnp.zeros((k_pad - nkeys, ncols), BF16)], axis=0)
        sink_term = jnp.exp2(sink_row - m)

        outs = []
        for g in range(kv_heads):
            cols = slice(g * gcols, (g + 1) * gcols)
            og = jnp.dot(vt_win[g * B_VT_ROWS:(g + 1) * B_VT_ROWS], p[:, cols],
                         preferred_element_type=F32)
            outs.append(og[:HEAD_DIM] / (og[HEAD_DIM:HEAD_DIM + 1] + sink_term[:, cols]))
        pairs = []
        for j in range(heads // 2):
            g, first = (2 * j) // group, (2 * j) % group
            pair = outs[g][:, first * BLOCK:(first + 2) * BLOCK]
            pairs.append(jnp.concatenate([pair[:, :BLOCK], pair[:, BLOCK:]], axis=0).T)
        o_ref[blk * BLOCK:(blk + 1) * BLOCK, :] = jnp.concatenate(pairs, axis=1).astype(BF16)

    pending = [block_scores(blk) for blk in range(min(B_SCORES_AHEAD, nblk))]
    for blk in range(nblk):
        if blk + B_SCORES_AHEAD < nblk:
            pending.append(block_scores(blk + B_SCORES_AHEAD))
        block_output(blk, pending.pop(0))


def _attn_b(pb, vtb, pb_meta, vtb_meta, sink, batch, seq, heads, kv_heads, nblk, slab):
    nb = seq // BLOCK
    n_steps = nb // nblk
    qw = heads * HEAD_DIM
    kcol = qw // LANES
    rows = kv_heads * B_VT_ROWS
    per_slab = slab // BLOCK
    assert kv_heads * HEAD_DIM == LANES and nb % nblk == 0 and per_slab % nblk == 0
    prev_blk = lambda i: jnp.maximum(nblk * i - 1, 0)
    next_blk = lambda i: jnp.minimum(nblk * i + nblk, nb - 1)
    vt_spec = lambda blk_of: pl.BlockSpec(
        (1, rows, BLOCK),
        lambda b, i: (b * (nb // per_slab) + blk_of(i) // per_slab, 0, blk_of(i) % per_slab))
    return pl.pallas_call(
        functools.partial(_attn_b_kernel, seq=seq, heads=heads, group=heads // kv_heads, nblk=nblk),
        grid=(batch, n_steps),
        in_specs=[
            pl.BlockSpec(memory_space=pltpu.SMEM),
            pl.BlockSpec((nblk * BLOCK, qw), lambda b, i: (b * n_steps + i, 0)),
            pl.BlockSpec((BLOCK, LANES), lambda b, i: (b * nb + prev_blk(i), kcol)),
            pl.BlockSpec((nblk * BLOCK, LANES), lambda b, i: (b * n_steps + i, kcol)),
            pl.BlockSpec((BLOCK, LANES), lambda b, i: (b * nb + next_blk(i), kcol)),
            vt_spec(prev_blk),
            pl.BlockSpec((1, rows, nblk * BLOCK),
                         lambda b, i: (b * (nb // per_slab) + (nblk * i) // per_slab, 0,
                                       i % (per_slab // nblk))),
            vt_spec(next_blk),
            _resident((META_PAD, LANES), lambda b, i: (0, kcol)),
            _resident((rows, META_PAD), lambda b, i: (0, 0)),
        ],
        out_specs=pl.BlockSpec((nblk * BLOCK, qw), lambda b, i: (b * n_steps + i, 0)),
        out_shape=jax.ShapeDtypeStruct((batch * seq, qw), BF16),
        scratch_shapes=[pltpu.VMEM((3, 3 * BLOCK + N_META, heads * BLOCK), F32)],
        compiler_params=pltpu.CompilerParams(
            dimension_semantics=("arbitrary", "arbitrary"),
            vmem_limit_bytes=VMEM_LIMIT_BYTES),
        name="attn_b",
    )(sink, pb, pb, pb, pb, vtb, vtb, vtb, pb_meta, vtb_meta)


def _rms(x, g):
    ms = jnp.mean(x * x, axis=-1, keepdims=True)
    return (x * lax.rsqrt(ms + EPS)) * g


def _post_kernel(x_ref, ya_ref, yb_ref, gate_ref, wba_ref, wbb_ref, wo_ref, nf_ref,
                 wg_ref, wu_ref, wd_ref, nfin_ref, o_ref, *, d_model):
    dot = functools.partial(jnp.dot, preferred_element_type=F32)
    rows = x_ref.shape[0] // POST_ROW_GROUPS
    groups = [slice(r * rows, (r + 1) * rows) for r in range(POST_ROW_GROUPS)]
    h1 = []
    for r in groups:
        ga = gate_ref[r, :d_model].astype(F32)
        gb = gate_ref[r, d_model:].astype(F32)
        merged = ga * dot(ya_ref[r, :], wba_ref[...]) + gb * dot(yb_ref[r, :], wbb_ref[...])
        h1.append(x_ref[r, :] + dot(merged.astype(BF16), wo_ref[...]))
    hn = [_rms(h, nf_ref[...]).astype(BF16) for h in h1]
    d_ff = wg_ref.shape[1]
    bounds = [min(k * FFN_COLUMN_GROUP, d_ff) for k in range(-(-d_ff // FFN_COLUMN_GROUP) + 1)]
    h2 = list(h1)
    for lo, hi in zip(bounds[:-1], bounds[1:]):
        for i in range(POST_ROW_GROUPS):
            act = jax.nn.silu(dot(hn[i], wg_ref[:, lo:hi])) * dot(hn[i], wu_ref[:, lo:hi])
            h2[i] = h2[i] + dot(act.astype(BF16), wd_ref[lo:hi, :])
    for r, h in zip(groups, h2):
        o_ref[r, :] = _rms(h, nfin_ref[...])


def _post(x2d, ya, yb, gates, wba, wbb, wo, nf, wg, wu, wd, nfin, tm):
    m, d = x2d.shape
    row = lambda w: pl.BlockSpec((tm, w), lambda i: (i, 0))
    full = lambda a: _resident(a.shape, lambda i: (0, 0))
    return pl.pallas_call(
        functools.partial(_post_kernel, d_model=d),
        grid=(m // tm,),
        in_specs=[row(d), row(ya.shape[1]), row(yb.shape[1]), row(2 * d),
                  full(wba), full(wbb), full(wo), full(nf),
                  full(wg), full(wu), full(wd), full(nfin)],
        out_specs=row(d),
        out_shape=jax.ShapeDtypeStruct((m, d), F32),
        compiler_params=pltpu.CompilerParams(
            dimension_semantics=("arbitrary",), vmem_limit_bytes=POST_VMEM_LIMIT_BYTES),
        name="post_ffn",
    )(x2d, ya, yb, gates, wba, wbb, wo, nf, wg, wu, wd, nfin)


def _extend_heads(vt_heads, first_row):
    heads, dim, n = vt_heads.shape
    extra = jnp.zeros((heads, BF16_SUBLANES, n), vt_heads.dtype).at[:, 0, :].set(first_row)
    return jnp.concatenate([vt_heads, extra], axis=1).reshape(heads * (dim + BF16_SUBLANES), n)


def kernel(x, meta_tokens, norm_mix, w_in, lambda_q1, lambda_k1, lambda_q2, lambda_k2,
           subln_gain, sink_logits, w_branch_a, w_branch_b, w_out, norm_ffn,
           w_ff_gate, w_ff_up, w_ff_down, norm_final):
    batch, seq, d = x.shape
    assert norm_mix.shape[0] == 1, "single-layer block"
    a_heads = w_branch_a.shape[1] // A_V_DIM
    assert 8 % a_heads == 0, "mixer A's bf16 position features need power-of-two ALiBi slopes"
    assert (seq - 1) >> _FEAT_SPLIT_BITS < 256, "position feature pieces must fit 8 bits"
    b_heads = sink_logits.shape[1]
    b_kv = max(1, b_heads // 4)
    qa_w = a_heads * 2 * HEAD_DIM
    qk_w = 2 * qa_w
    va_w = a_heads * A_V_DIM
    qb_w = b_heads * HEAD_DIM
    kvb_w = b_kv * HEAD_DIM
    pb_lo = qk_w + va_w
    vb_lo = pb_lo + qb_w + kvb_w
    gates_from = vb_lo + kvb_w

    x2d = x.reshape(batch * seq, d)
    col_scale = (jnp.ones((w_in.shape[2],), F32)
                 .at[:qa_w].set(SCALE * LOG2E).at[pb_lo:pb_lo + qb_w].set(SCALE * LOG2E))
    w_in_b = (w_in[0] * col_scale).astype(BF16)
    wvt_a = _extend_heads(w_in_b[:, qk_w:pb_lo].T.reshape(a_heads, A_V_DIM, d), 0)
    wvt_b = _extend_heads(w_in_b[:, vb_lo:gates_from].T.reshape(b_kv, HEAD_DIM, d), 0)
    wvt = jnp.concatenate([wvt_a, wvt_b], axis=0)
    na, nbt = wvt_a.shape[0], wvt_b.shape[0]
    ones_rows = (tuple(range(A_V_DIM, na, A_VT_ROWS))
                 + tuple(range(na + HEAD_DIM, na + nbt, B_VT_ROWS)))
    gain = norm_mix[0][None, :]

    pa, pb, vta, vtb, gates = _in_proj(
        x2d, gain, w_in_b, wvt, ((0, qk_w), (pb_lo, vb_lo)), gates_from,
        ((0, na, A_KEY_CHUNK), (na, na + nbt, IN_PROJ_ROW_TILE)), ones_rows, tm=IN_PROJ_ROW_TILE)
    pa_m, va_m, pb_m, vb_m = _in_proj(
        meta_tokens.astype(F32), gain, w_in_b, wvt,
        ((0, qk_w), (qk_w, pb_lo), (pb_lo, vb_lo), (vb_lo, gates_from)), None, (), (), tm=N_META)
    pad = ((0, META_PAD - N_META), (0, 0))
    pa_m = jnp.pad(pa_m, pad)
    pb_m = jnp.pad(pb_m, pad)
    vta_m = _extend_heads(jnp.pad(va_m, pad).T.reshape(a_heads, A_V_DIM, META_PAD), 1)
    vtb_m = _extend_heads(jnp.pad(vb_m, pad).T.reshape(b_kv, HEAD_DIM, META_PAD), 1)

    slopes = jnp.asarray([2.0 ** (-8.0 * (i + 1) / a_heads) for i in range(a_heads)], F32)
    ya = _attn_a(pa, vta, pa_m, vta_m, slopes, lambda_q1, lambda_k1, lambda_q2, lambda_k2,
                 subln_gain[0][:, None], batch, seq, a_heads,
                 tq=A_QUERY_TILE, tk=A_KEY_CHUNK, hps=A_HEADS_PER_STEP)
    yb = _attn_b(pb, vtb, pb_m, vtb_m, sink_logits[0], batch, seq, b_heads, b_kv,
                 nblk=B_BLOCKS_PER_STEP, slab=IN_PROJ_ROW_TILE)

    out = _post(x2d, ya, yb, gates,
                w_branch_a[0].astype(BF16), w_branch_b[0].astype(BF16), w_out[0].astype(BF16),
                norm_ffn[0][None, :], w_ff_gate[0].astype(BF16), w_ff_up[0].astype(BF16),
                w_ff_down[0].astype(BF16), norm_final[None, :], tm=POST_ROW_TILE)
    return out.reshape(batch, seq, d)
```

```python
import functools
import math

import ml_dtypes
import numpy as np

import jax
import jax.numpy as jnp
from jax import lax
from jax.experimental import pallas as pl
from jax.experimental.pallas import tpu as pltpu

F32 = jnp.float32
BF16 = jnp.bfloat16

N_META = 16
HEAD_DIM = 64
WINDOW = 128
BLOCK = 128
A_V_DIM = 2 * HEAD_DIM
EPS = 1e-6
LAMBDA_INIT = 0.8 - 0.6 * math.exp(-0.3 * 0)
SCALE = HEAD_DIM ** -0.5
LOG2E = math.log2(math.e)

LANES = 128
BF16_SUBLANES = 16
META_PAD = 128
NEG = -1e30
VMEM_LIMIT_BYTES = 56 * 1024 * 1024
POST_VMEM_LIMIT_BYTES = 62 * 1024 * 1024

IN_PROJ_ROW_TILE = 1024
A_KEY_CHUNK = 512
A_QUERY_TILE = 1024
A_SCORES_AHEAD = 1
A_HEADS_PER_STEP = 2
POST_ROW_TILE = 1024
POST_ROW_GROUPS = 4
FFN_COLUMN_GROUP = 1536

A_VT_ROWS = A_V_DIM + BF16_SUBLANES
B_VT_ROWS = HEAD_DIM + BF16_SUBLANES
B_BLOCKS_PER_STEP = 8
B_SCORES_AHEAD = 2
_BLOCK_BITS = BLOCK.bit_length() - 1

_DN_NT = (((1,), (1,)), ((), ()))


def _resident(block_shape, index_map):
    return pl.BlockSpec(block_shape, index_map, pipeline_mode=pl.Buffered(1))


def _in_proj_kernel(x_ref, g_ref, w_ref, wvt_ref, *out_refs, splits, gates_from, vt_splits,
                    ones_rows):
    x = x_ref[...]
    ms = jnp.mean(x * x, axis=-1, keepdims=True)
    hn = ((x * lax.rsqrt(ms + EPS)) * g_ref[...]).astype(BF16)
    if gates_from is not None:
        g = jnp.dot(hn, w_ref[:, gates_from:], preferred_element_type=F32)
        out_refs[-1][...] = jax.nn.sigmoid(g).astype(BF16)
    for (lo, hi), o_ref in zip(splits, out_refs):
        o_ref[...] = jnp.dot(hn, w_ref[:, lo:hi], preferred_element_type=F32).astype(BF16)
    if gates_from is not None:
        vt_refs = out_refs[len(splits):-1]
        vt = lax.dot_general(wvt_ref[...], hn, _DN_NT, preferred_element_type=F32)
        row = lax.broadcasted_iota(jnp.int32, vt.shape, 0)
        is_ones = functools.reduce(jnp.logical_or, [row == r for r in ones_rows])
        vt = jnp.where(is_ones, 1.0, vt).astype(BF16)
        for (lo, hi, slab), vt_ref in zip(vt_splits, vt_refs):
            for j in range(vt.shape[1] // slab):
                vt_ref[j] = vt[lo:hi, j * slab:(j + 1) * slab]


def _in_proj(x2d, gain, w_bf16, wvt_bf16, splits, gates_from, vt_splits, ones_rows, tm):
    m, d = x2d.shape
    n_all = w_bf16.shape[1]
    nv = wvt_bf16.shape[0]
    widths = [hi - lo for lo, hi in splits]
    out_shape = [jax.ShapeDtypeStruct((m, w), BF16) for w in widths]
    out_specs = [pl.BlockSpec((tm, w), lambda i: (i, 0)) for w in widths]
    if gates_from is not None:
        for lo, hi, slab in vt_splits:
            out_shape.append(jax.ShapeDtypeStruct((m // slab, hi - lo, slab), BF16))
            out_specs.append(pl.BlockSpec((tm // slab, hi - lo, slab), lambda i: (i, 0, 0)))
        out_shape.append(jax.ShapeDtypeStruct((m, n_all - gates_from), BF16))
        out_specs.append(pl.BlockSpec((tm, n_all - gates_from), lambda i: (i, 0)))
    return pl.pallas_call(
        functools.partial(_in_proj_kernel, splits=splits, gates_from=gates_from,
                          vt_splits=vt_splits, ones_rows=ones_rows),
        grid=(m // tm,),
        in_specs=[
            pl.BlockSpec((tm, d), lambda i: (i, 0)),
            _resident((1, d), lambda i: (0, 0)),
            _resident((d, n_all), lambda i: (0, 0)),
            _resident((nv, d), lambda i: (0, 0)),
        ],
        out_specs=out_specs,
        out_shape=out_shape,
        compiler_params=pltpu.CompilerParams(
            dimension_semantics=("arbitrary",), vmem_limit_bytes=VMEM_LIMIT_BYTES,
            allow_input_fusion=[False, False, True, True]),
        name="in_proj_meta" if gates_from is None else "in_proj_gates",
    )(x2d, gain, w_bf16, wvt_bf16)


_FEAT_SPLIT_BITS = 6
_N_PIECES = 3


def _bf16_pieces(x, n):
    pieces, rest = [], np.float32(x)
    for _ in range(n):
        piece = np.float32(np.asarray(rest, dtype=ml_dtypes.bfloat16))
        pieces.append(piece)
        rest = np.float32(rest - piece)
    return pieces


def _feature_table():
    e = _bf16_pieces(LOG2E, _N_PIECES)
    t = np.zeros((8, LANES), np.float32)
    n = _N_PIECES
    t[0, 0:n] = 1.0
    t[1, n:2 * n] = 1.0
    t[2, 2 * n:3 * n] = e
    t[2, 3 * n:4 * n] = e
    t[3, 2 * n:3 * n] = 1.0
    t[4, 3 * n:4 * n] = 1.0
    t[5, 0:n] = e
    t[5, n:2 * n] = e
    return jnp.asarray(t)


def _pos_terms(pos, slope):
    hi = (pos >> _FEAT_SPLIT_BITS).astype(F32) * (slope * (1 << _FEAT_SPLIT_BITS))
    lo = (pos & ((1 << _FEAT_SPLIT_BITS) - 1)).astype(F32) * slope
    return hi, lo


def _attn_a_kernel(slopes_ref, lq1_ref, lk1_ref, lq2_ref, lk2_ref, gain_ref, feat_ref,
                   q_ref, k_ref, vt_ref, km_ref, vtm_ref, o_ref,
                   kaug_ref, qfeat_ref, negabs_ref, *, tq, tk, seq, hps):
    b = pl.program_id(0)
    hp = pl.program_id(1)
    qi = pl.program_id(2)
    nqc = 2 * tq
    n_chunks = seq // tk
    n_off = max(1, tk // tq)
    n_diag = max(1, tq // tk)

    @pl.when((b == 0) & (hp == 0) & (qi == 0))
    def _build_distance_tiles():
        for v in range(max(n_off, n_diag)):
            start = v * tq if tk >= tq else -v * tk
            d = ((lax.broadcasted_iota(jnp.int32, (tk, nqc), 1) & (tq - 1))
                 - lax.broadcasted_iota(jnp.int32, (tk, nqc), 0) + start)
            negabs_ref[v] = -jnp.abs(d).astype(F32)

    @pl.when(qi == 0)
    def _build_features():
        def fill(c, carry):
            rows = pl.ds(pl.multiple_of(c * tk, tk), tk)
            pos = c * tk + lax.broadcasted_iota(jnp.int32, (tk, LANES), 0)
            for j in range(hps):
                hi, lo = _pos_terms(pos, slopes_ref[hp * hps + j])
                kaug_ref[j, rows, :LANES] = k_ref[rows, j * LANES:(j + 1) * LANES]
                kf = hi * feat_ref[0:1, :] + lo * feat_ref[1:2, :] + feat_ref[2:3, :]
                kaug_ref[j, rows, LANES:] = kf.astype(BF16)
                qf = feat_ref[5:6, :] - hi * feat_ref[3:4, :] - lo * feat_ref[4:5, :]
                qfeat_ref[j, rows, :] = qf.astype(BF16)
            return carry
        lax.fori_loop(0, n_chunks, fill, 0)

    lam = (jnp.exp(jnp.sum(lq1_ref[...] * lk1_ref[...], axis=-1, keepdims=True))
           - jnp.exp(jnp.sum(lq2_ref[...] * lk2_ref[...], axis=-1, keepdims=True))
           + LAMBDA_INIT)
    lane = lax.broadcasted_iota(jnp.int32, (tq, LANES), 1)
    meta_valid = lax.broadcasted_iota(jnp.int32, (META_PAD, nqc), 0) < N_META
    c_diag = (qi * tq) // tk

    def scores(k_rows, q_aug_t):
        return jnp.dot(k_rows, q_aug_t, preferred_element_type=F32)

    def softmax(s, m):
        m_new = jnp.maximum(m, jnp.max(s, axis=0, keepdims=True))
        return m_new, jnp.exp2(s - m_new).astype(BF16), jnp.exp2(m - m_new)

    def chunk_of(t):
        if t < n_diag:
            return c_diag + t
        u = t - n_diag
        return u + (u >= c_diag).astype(jnp.int32) * n_diag

    def one_head(j):
        slope = slopes_ref[hp * hps + j]
        vt_rows = slice(j * A_VT_ROWS, (j + 1) * A_VT_ROWS)
        q = q_ref[:, j * LANES:(j + 1) * LANES]
        zero = jnp.zeros_like(q)
        qq = jnp.concatenate([jnp.where(lane < HEAD_DIM, q, zero),
                              jnp.where(lane >= HEAD_DIM, q, zero)], axis=0)
        qf = qfeat_ref[j, pl.ds(pl.multiple_of(qi * tq, tq), tq), :]
        qf = jnp.concatenate([qf, qf], axis=0)
        qq_t, qf_t = qq.T, qf.T
        q_left = jnp.concatenate([qq_t, qf_t], axis=0)
        q_right = jnp.concatenate([qq_t, -qf_t], axis=0)
        q_plain = jnp.concatenate([qq_t, jnp.zeros_like(qf_t)], axis=0)

        def chunk_scores(t):
            c = chunk_of(t)
            k_rows = kaug_ref[j, pl.ds(pl.multiple_of(c * tk, tk), tk), :]
            if t < n_diag:
                variant = qi - c_diag * n_off if tk >= tq else t
                return scores(k_rows, q_plain) + negabs_ref[variant] * (slope * LOG2E)
            return scores(k_rows, jnp.where(c < c_diag, q_left, q_right))

        km_aug = jnp.concatenate([km_ref[:, j * LANES:(j + 1) * LANES],
                                  jnp.zeros((META_PAD, LANES), BF16)], axis=1)
        s = jnp.where(meta_valid, scores(km_aug, q_plain), NEG)
        m = jnp.max(s, axis=0, keepdims=True)
        acc = jnp.dot(vtm_ref[vt_rows, :], jnp.exp2(s - m).astype(BF16),
                      preferred_element_type=F32)

        pending = [chunk_scores(t) for t in range(min(A_SCORES_AHEAD, n_chunks))]
        for t in range(n_chunks):
            if t + A_SCORES_AHEAD < n_chunks:
                pending.append(chunk_scores(t + A_SCORES_AHEAD))
            m, p, alpha = softmax(pending.pop(0), m)
            acc = alpha * acc + jnp.dot(vt_ref[chunk_of(t), vt_rows, :], p,
                                        preferred_element_type=F32)

        o = acc[:A_V_DIM] / acc[A_V_DIM:A_V_DIM + 1]
        y = o[:, :tq] - lam * o[:, tq:]
        ms = jnp.mean(y * y, axis=0, keepdims=True)
        y = ((y * lax.rsqrt(ms + EPS)) * gain_ref[...]) * (1.0 - LAMBDA_INIT)
        o_ref[:, j * A_V_DIM:(j + 1) * A_V_DIM] = y.T.astype(BF16)

    for j in range(hps):
        one_head(j)


def _attn_a(pa, vt, pa_meta, vt_meta, slopes, lq1, lk1, lq2, lk2, gain_col,
            batch, seq, heads, tq, tk, hps):
    nq = seq // tq
    width = heads * A_V_DIM
    assert max(tk, tq) % min(tk, tq) == 0 and seq % tk == 0 and seq % tq == 0
    assert heads % hps == 0
    kcol = heads // hps
    vec = lambda n: _resident((1, n), lambda b, h, i: (0, 0))
    return pl.pallas_call(
        functools.partial(_attn_a_kernel, tq=tq, tk=tk, seq=seq, hps=hps),
        grid=(batch, heads // hps, nq),
        in_specs=[
            pl.BlockSpec(memory_space=pltpu.SMEM),
            vec(HEAD_DIM), vec(HEAD_DIM), vec(HEAD_DIM), vec(HEAD_DIM),
            _resident((A_V_DIM, 1), lambda b, h, i: (0, 0)),
            _resident((8, LANES), lambda b, h, i: (0, 0)),
            pl.BlockSpec((tq, hps * LANES), lambda b, h, i: (b * nq + i, h)),
            pl.BlockSpec((seq, hps * LANES), lambda b, h, i: (b, kcol + h)),
            pl.BlockSpec((seq // tk, hps * A_VT_ROWS, tk), lambda b, h, i: (b, h, 0)),
            pl.BlockSpec((META_PAD, hps * LANES), lambda b, h, i: (0, kcol + h)),
            pl.BlockSpec((hps * A_VT_ROWS, META_PAD), lambda b, h, i: (h, 0)),
        ],
        out_specs=pl.BlockSpec((tq, hps * A_V_DIM), lambda b, h, i: (b * nq + i, h)),
        out_shape=jax.ShapeDtypeStruct((batch * seq, width), BF16),
        scratch_shapes=[pltpu.VMEM((hps, seq, 2 * LANES), BF16),
                        pltpu.VMEM((hps, seq, LANES), BF16),
                        pltpu.VMEM((max(tk // tq, tq // tk), tk, 2 * tq), F32)],
        compiler_params=pltpu.CompilerParams(
            dimension_semantics=("arbitrary", "arbitrary", "arbitrary"),
            vmem_limit_bytes=VMEM_LIMIT_BYTES),
        name="attn_a",
    )(slopes, lq1, lk1, lq2, lk2, gain_col, _feature_table(), pa, pa, vt, pa_meta, vt_meta)


def _attn_b_kernel(sink_ref, q_ref, kprev_ref, kcur_ref, knext_ref, vprev_ref, vcur_ref,
                   vnext_ref, km_ref, vtm_ref, o_ref, bias_ref, *, seq, heads, group, nblk):
    b = pl.program_id(0)
    step = pl.program_id(1)
    nb = seq // BLOCK
    kv_heads = heads // group
    nwin = 3 * BLOCK
    nkeys = nwin + N_META
    ncols = heads * BLOCK
    gcols = group * BLOCK
    k_pad = -(-nkeys // (2 * LANES)) * (2 * LANES)

    @pl.when((b == 0) & (step == 0))
    def _build_bias():
        kr = lax.broadcasted_iota(jnp.int32, (nkeys, ncols), 0)
        col = lax.broadcasted_iota(jnp.int32, (nkeys, ncols), 1)
        dt = jnp.abs((col & (BLOCK - 1)) - (kr - BLOCK))
        head = (col >> _BLOCK_BITS).astype(F32)
        slope = jnp.exp2(-(head + 1.0) * (8.0 / heads))
        base = jnp.where((kr < nwin) & (dt <= WINDOW), dt.astype(F32) * (-LOG2E) * slope,
                         jnp.where(kr >= nwin, 0.0, NEG))
        bias_ref[0] = base
        bias_ref[1] = jnp.where(kr < BLOCK, NEG, base)
        bias_ref[2] = jnp.where((kr >= 2 * BLOCK) & (kr < nwin), NEG, base)

    col_head = lax.broadcasted_iota(jnp.int32, (1, ncols), 1) >> _BLOCK_BITS
    sink_row = jnp.zeros((1, ncols), F32)
    for hd in range(heads):
        sink_row = jnp.where(col_head == hd, sink_ref[hd] * LOG2E, sink_row)

    lane = lax.broadcasted_iota(jnp.int32, (BLOCK, LANES), 1)
    lo = lane < HEAD_DIM
    zero = jnp.zeros((BLOCK, LANES), BF16)
    k_blocks = ([kprev_ref[...]]
                + [kcur_ref[j * BLOCK:(j + 1) * BLOCK, :] for j in range(nblk)] + [knext_ref[...]])
    vt_blocks = ([vprev_ref[0]]
                 + [vcur_ref[0, :, j * BLOCK:(j + 1) * BLOCK] for j in range(nblk)] + [vnext_ref[0]])
    k_meta = km_ref[:N_META, :]
    vt_meta = vtm_ref[...]

    def block_scores(blk):
        i = step * nblk + blk
        k_win = jnp.concatenate(k_blocks[blk:blk + 3] + [k_meta], axis=0)

        q = q_ref[blk * BLOCK:(blk + 1) * BLOCK, :]
        q_heads = []
        for hd in range(heads):
            part = q[:, (hd // 2) * LANES:(hd // 2 + 1) * LANES]
            in_lo, want_lo = hd % 2 == 0, (hd // group) % 2 == 0
            if in_lo != want_lo:
                part = jnp.concatenate([part[:, HEAD_DIM:], part[:, :HEAD_DIM]], axis=1)
            q_heads.append(jnp.where(lo if want_lo else ~lo, part, zero))
        q_all = jnp.concatenate(q_heads, axis=0)

        variant = jnp.where(i == 0, 1, jnp.where(i == nb - 1, 2, 0))
        return (lax.dot_general(k_win, q_all, _DN_NT, preferred_element_type=F32)
                + bias_ref[variant])

    def block_output(blk, s):
        vt_win = jnp.concatenate(vt_blocks[blk:blk + 3] + [vt_meta], axis=1)
        m = jnp.maximum(jnp.max(s, axis=0, keepdims=True), sink_row)
        p = jnp.exp2(s - m).astype(BF16)
        p = jnp.concatenate([p, jnp.zeros((k_pad - nkeys, ncols), BF16)], axis=0)
        sink_term = jnp.exp2(sink_row - m)

        outs = []
        for g in range(kv_heads):
            cols = slice(g * gcols, (g + 1) * gcols)
            og = jnp.dot(vt_win[g * B_VT_ROWS:(g + 1) * B_VT_ROWS], p[:, cols],
                         preferred_element_type=F32)
            outs.append(og[:HEAD_DIM] / (og[HEAD_DIM:HEAD_DIM + 1] + sink_term[:, cols]))
        pairs = []
        for j in range(heads // 2):
            g, first = (2 * j) // group, (2 * j) % group
            pair = outs[g][:, first * BLOCK:(first + 2) * BLOCK]
            pairs.append(jnp.concatenate([pair[:, :BLOCK], pair[:, BLOCK:]], axis=0).T)
        o_ref[blk * BLOCK:(blk + 1) * BLOCK, :] = jnp.concatenate(pairs, axis=1).astype(BF16)

    pending = [block_scores(blk) for blk in range(min(B_SCORES_AHEAD, nblk))]
    for blk in range(nblk):
        if blk + B_SCORES_AHEAD < nblk:
            pending.append(block_scores(blk + B_SCORES_AHEAD))
        block_output(blk, pending.pop(0))


def _attn_b(pb, vtb, pb_meta, vtb_meta, sink, batch, seq, heads, kv_heads, nblk, slab):
    nb = seq // BLOCK
    n_steps = nb // nblk
    qw = heads * HEAD_DIM
    kcol = qw // LANES
    rows = kv_heads * B_VT_ROWS
    per_slab = slab // BLOCK
    assert kv_heads * HEAD_DIM == LANES and nb % nblk == 0 and per_slab % nblk == 0
    prev_blk = lambda i: jnp.maximum(nblk * i - 1, 0)
    next_blk = lambda i: jnp.minimum(nblk * i + nblk, nb - 1)
    vt_spec = lambda blk_of: pl.BlockSpec(
        (1, rows, BLOCK),
        lambda b, i: (b * (nb // per_slab) + blk_of(i) // per_slab, 0, blk_of(i) % per_slab))
    return pl.pallas_call(
        functools.partial(_attn_b_kernel, seq=seq, heads=heads, group=heads // kv_heads, nblk=nblk),
        grid=(batch, n_steps),
        in_specs=[
            pl.BlockSpec(memory_space=pltpu.SMEM),
            pl.BlockSpec((nblk * BLOCK, qw), lambda b, i: (b * n_steps + i, 0)),
            pl.BlockSpec((BLOCK, LANES), lambda b, i: (b * nb + prev_blk(i), kcol)),
            pl.BlockSpec((nblk * BLOCK, LANES), lambda b, i: (b * n_steps + i, kcol)),
            pl.BlockSpec((BLOCK, LANES), lambda b, i: (b * nb + next_blk(i), kcol)),
            vt_spec(prev_blk),
            pl.BlockSpec((1, rows, nblk * BLOCK),
                         lambda b, i: (b * (nb // per_slab) + (nblk * i) // per_slab, 0,
                                       i % (per_slab // nblk))),
            vt_spec(next_blk),
            _resident((META_PAD, LANES), lambda b, i: (0, kcol)),
            _resident((rows, META_PAD), lambda b, i: (0, 0)),
        ],
        out_specs=pl.BlockSpec((nblk * BLOCK, qw), lambda b, i: (b * n_steps + i, 0)),
        out_shape=jax.ShapeDtypeStruct((batch * seq, qw), BF16),
        scratch_shapes=[pltpu.VMEM((3, 3 * BLOCK + N_META, heads * BLOCK), F32)],
        compiler_params=pltpu.CompilerParams(
            dimension_semantics=("arbitrary", "arbitrary"),
            vmem_limit_bytes=VMEM_LIMIT_BYTES),
        name="attn_b",
    )(sink, pb, pb, pb, pb, vtb, vtb, vtb, pb_meta, vtb_meta)


def _rms(x, g):
    ms = jnp.mean(x * x, axis=-1, keepdims=True)
    return (x * lax.rsqrt(ms + EPS)) * g


def _post_kernel(x_ref, ya_ref, yb_ref, gate_ref, wba_ref, wbb_ref, wo_ref, nf_ref,
                 wg_ref, wu_ref, wd_ref, nfin_ref, o_ref, *, d_model):
    dot = functools.partial(jnp.dot, preferred_element_type=F32)
    rows = x_ref.shape[0] // POST_ROW_GROUPS
    groups = [slice(r * rows, (r + 1) * rows) for r in range(POST_ROW_GROUPS)]
    h1 = []
    for r in groups:
        ga = gate_ref[r, :d_model].astype(F32)
        gb = gate_ref[r, d_model:].astype(F32)
        merged = ga * dot(ya_ref[r, :], wba_ref[...]) + gb * dot(yb_ref[r, :], wbb_ref[...])
        h1.append(x_ref[r, :] + dot(merged.astype(BF16), wo_ref[...]))
    hn = [_rms(h, nf_ref[...]).astype(BF16) for h in h1]
    d_ff = wg_ref.shape[1]
    bounds = [min(k * FFN_COLUMN_GROUP, d_ff) for k in range(-(-d_ff // FFN_COLUMN_GROUP) + 1)]
    h2 = list(h1)
    for lo, hi in zip(bounds[:-1], bounds[1:]):
        for i in range(POST_ROW_GROUPS):
            act = jax.nn.silu(dot(hn[i], wg_ref[:, lo:hi])) * dot(hn[i], wu_ref[:, lo:hi])
            h2[i] = h2[i] + dot(act.astype(BF16), wd_ref[lo:hi, :])
    for r, h in zip(groups, h2):
        o_ref[r, :] = _rms(h, nfin_ref[...])


def _post(x2d, ya, yb, gates, wba, wbb, wo, nf, wg, wu, wd, nfin, tm):
    m, d = x2d.shape
    row = lambda w: pl.BlockSpec((tm, w), lambda i: (i, 0))
    full = lambda a: _resident(a.shape, lambda i: (0, 0))
    return pl.pallas_call(
        functools.partial(_post_kernel, d_model=d),
        grid=(m // tm,),
        in_specs=[row(d), row(ya.shape[1]), row(yb.shape[1]), row(2 * d),
                  full(wba), full(wbb), full(wo), full(nf),
                  full(wg), full(wu), full(wd), full(nfin)],
        out_specs=row(d),
        out_shape=jax.ShapeDtypeStruct((m, d), F32),
        compiler_params=pltpu.CompilerParams(
            dimension_semantics=("arbitrary",), vmem_limit_bytes=POST_VMEM_LIMIT_BYTES,
            allow_input_fusion=[i in (4, 5, 6, 8, 9, 10) for i in range(12)]),
        name="post_ffn",
    )(x2d, ya, yb, gates, wba, wbb, wo, nf, wg, wu, wd, nfin)


def _extend_heads(vt_heads, first_row):
    heads, dim, n = vt_heads.shape
    extra = jnp.zeros((heads, BF16_SUBLANES, n), vt_heads.dtype).at[:, 0, :].set(first_row)
    return jnp.concatenate([vt_heads, extra], axis=1).reshape(heads * (dim + BF16_SUBLANES), n)


def kernel(x, meta_tokens, norm_mix, w_in, lambda_q1, lambda_k1, lambda_q2, lambda_k2,
           subln_gain, sink_logits, w_branch_a, w_branch_b, w_out, norm_ffn,
           w_ff_gate, w_ff_up, w_ff_down, norm_final):
    batch, seq, d = x.shape
    assert norm_mix.shape[0] == 1, "single-layer block"
    a_heads = w_branch_a.shape[1] // A_V_DIM
    assert 8 % a_heads == 0, "mixer A's bf16 position features need power-of-two ALiBi slopes"
    assert (seq - 1) >> _FEAT_SPLIT_BITS < 256, "position feature pieces must fit 8 bits"
    b_heads = sink_logits.shape[1]
    b_kv = max(1, b_heads // 4)
    qa_w = a_heads * 2 * HEAD_DIM
    qk_w = 2 * qa_w
    va_w = a_heads * A_V_DIM
    qb_w = b_heads * HEAD_DIM
    kvb_w = b_kv * HEAD_DIM
    pb_lo = qk_w + va_w
    vb_lo = pb_lo + qb_w + kvb_w
    gates_from = vb_lo + kvb_w

    x2d = x.reshape(batch * seq, d)
    col_scale = (jnp.ones((w_in.shape[2],), F32)
                 .at[:qa_w].set(SCALE * LOG2E).at[pb_lo:pb_lo + qb_w].set(SCALE * LOG2E))
    w_in_b = (w_in[0] * col_scale).astype(BF16)
    wvt_a = _extend_heads(w_in_b[:, qk_w:pb_lo].T.reshape(a_heads, A_V_DIM, d), 0)
    wvt_b = _extend_heads(w_in_b[:, vb_lo:gates_from].T.reshape(b_kv, HEAD_DIM, d), 0)
    wvt = jnp.concatenate([wvt_a, wvt_b], axis=0)
    na, nbt = wvt_a.shape[0], wvt_b.shape[0]
    ones_rows = (tuple(range(A_V_DIM, na, A_VT_ROWS))
                 + tuple(range(na + HEAD_DIM, na + nbt, B_VT_ROWS)))
    gain = norm_mix[0][None, :]

    pa, pb, vta, vtb, gates = _in_proj(
        x2d, gain, w_in_b, wvt, ((0, qk_w), (pb_lo, vb_lo)), gates_from,
        ((0, na, A_KEY_CHUNK), (na, na + nbt, IN_PROJ_ROW_TILE)), ones_rows, tm=IN_PROJ_ROW_TILE)
    pa_m, va_m, pb_m, vb_m = _in_proj(
        meta_tokens.astype(F32), gain, w_in_b, wvt,
        ((0, qk_w), (qk_w, pb_lo), (pb_lo, vb_lo), (vb_lo, gates_from)), None, (), (), tm=N_META)
    pad = ((0, META_PAD - N_META), (0, 0))
    pa_m = jnp.pad(pa_m, pad)
    pb_m = jnp.pad(pb_m, pad)
    vta_m = _extend_heads(jnp.pad(va_m, pad).T.reshape(a_heads, A_V_DIM, META_PAD), 1)
    vtb_m = _extend_heads(jnp.pad(vb_m, pad).T.reshape(b_kv, HEAD_DIM, META_PAD), 1)

    slopes = jnp.asarray([2.0 ** (-8.0 * (i + 1) / a_heads) for i in range(a_heads)], F32)
    ya = _attn_a(pa, vta, pa_m, vta_m, slopes, lambda_q1, lambda_k1, lambda_q2, lambda_k2,
                 subln_gain[0][:, None], batch, seq, a_heads,
                 tq=A_QUERY_TILE, tk=A_KEY_CHUNK, hps=A_HEADS_PER_STEP)
    yb = _attn_b(pb, vtb, pb_m, vtb_m, sink_logits[0], batch, seq, b_heads, b_kv,
                 nblk=B_BLOCKS_PER_STEP, slab=IN_PROJ_ROW_TILE)

    out = _post(x2d, ya, yb, gates,
                w_branch_a[0].astype(BF16), w_branch_b[0].astype(BF16), w_out[0].astype(BF16),
                norm_ffn[0][None, :], w_ff_gate[0].astype(BF16), w_ff_up[0].astype(BF16),
                w_ff_down[0].astype(BF16), norm_final[None, :], tm=POST_ROW_TILE)
    return out.reshape(batch, seq, d)
```
